```python
import math
import jax, jax.numpy as jnp
from jax import lax
import numpy as np

D_MODEL = 1024
BATCH = 8
SEQ = 2048
DEPTH = 2
DEC_BATCH = 128
DEC_SEQ = 8
PAST_LEN = 16384
PAGE_SIZE = 128

BRANCH_W = D_MODEL // 2
N_BRANCH = 3
N_POOL_GROUPS = 4
POOL_WINDOWS = (2, 4, 8, 16)
POOL_GROUP_W = BRANCH_W // N_POOL_GROUPS
POOL_BUF = max(POOL_WINDOWS) - 1
CONV_WIDTH = 31
CONV_BUF = CONV_WIDTH - 1
N_MEM = 256
N_XHEADS = 4
XHEAD_DIM = BRANCH_W // N_XHEADS
N_IN_SLICES = 7
IN_COLS = N_IN_SLICES * BRANCH_W + N_BRANCH * D_MODEL
EPS = 1e-6

kernel_name = "hybrid_pool_conv_memattn_decoder_step"


def rmsnorm(x, g):
    xf = x.astype(jnp.float32)
    r = xf * lax.rsqrt(jnp.mean(xf * xf, axis=-1, keepdims=True) + EPS)
    return (r * g.astype(jnp.float32)).astype(x.dtype)


def layernorm(x, g, b):
    xf = x.astype(jnp.float32)
    mu = jnp.mean(xf, axis=-1, keepdims=True)
    var = jnp.mean(jnp.square(xf - mu), axis=-1, keepdims=True)
    r = (xf - mu) * lax.rsqrt(var + EPS)
    return (r * g.astype(jnp.float32) + b.astype(jnp.float32)).astype(x.dtype)


def pool_mix(ext, pos0, pool_w, pool_scale):
    B, L, W = ext.shape
    S = L - POOL_BUF
    ef = ext.astype(jnp.float32)
    csum = jnp.concatenate([jnp.zeros((B, 1, W), jnp.float32), jnp.cumsum(ef, axis=1)], axis=1)
    pos = (pos0 + jnp.arange(S)).astype(jnp.float32)
    parts = []
    for g, win in enumerate(POOL_WINDOWS):
        cg = csum[..., g * POOL_GROUP_W:(g + 1) * POOL_GROUP_W]
        hi = cg[:, POOL_BUF + 1:POOL_BUF + 1 + S]
        lo = cg[:, POOL_BUF + 1 - win:POOL_BUF + 1 - win + S]
        cnt = jnp.minimum(pos + 1.0, float(win))[None, :, None]
        parts.append((hi - lo) / cnt)
    pooled = jnp.concatenate(parts, axis=-1)
    mixed = (pooled - ef[:, POOL_BUF:]).reshape(B, S, N_POOL_GROUPS, POOL_GROUP_W)
    y = jnp.einsum('bsgc,gcd->bsgd', mixed, pool_w.astype(jnp.float32)).reshape(B, S, W)
    return (y * pool_scale.astype(jnp.float32)).astype(ext.dtype)


def causal_dwconv(ext, conv_w, conv_b):
    W = ext.shape[-1]
    out = lax.conv_general_dilated(ext, conv_w[:, None, :].astype(ext.dtype), window_strides=(1,),
                                   padding='VALID', dimension_numbers=('NWC', 'WIO', 'NWC'),
                                   feature_group_count=W)
    return out + conv_b


def mem_kv(mem, g, w):
    B = mem.shape[0]
    kv = rmsnorm(mem, g) @ w
    k = kv[..., :BRANCH_W].reshape(B, N_MEM, N_XHEADS, XHEAD_DIM)
    v = kv[..., BRANCH_W:].reshape(B, N_MEM, N_XHEADS, XHEAD_DIM)
    return k, v


def layer(x, pool_buf, conv_buf, mk, mv, pos0, g_pre, g_post, w_in, pool_w, pool_scale,
          conv_w, conv_b, conv_ln_g, conv_ln_b, w_branch, w_out):
    B, S, _ = x.shape
    W = BRANCH_W
    h = rmsnorm(x, g_pre)
    proj = h @ w_in
    p_in, p_gate, c_val, c_glu, c_gate, q, x_gate = [proj[..., i * W:(i + 1) * W] for i in range(N_IN_SLICES)]
    merge_logits = proj[..., N_IN_SLICES * W:].reshape(B, S, N_BRANCH, D_MODEL)
    pool_ext = jnp.concatenate([pool_buf, p_in], axis=1)
    a = pool_mix(pool_ext, pos0, pool_w, pool_scale) * jax.nn.silu(p_gate)
    u = c_val * jax.nn.sigmoid(c_glu)
    conv_ext = jnp.concatenate([conv_buf, u], axis=1)
    cv = jax.nn.silu(layernorm(causal_dwconv(conv_ext, conv_w, conv_b), conv_ln_g, conv_ln_b))
    bconv = cv * jax.nn.silu(c_gate)
    qh = q.reshape(B, S, N_XHEADS, XHEAD_DIM)
    s = jnp.einsum('bshd,bmhd->bhsm', qh, mk).astype(jnp.float32) / math.sqrt(XHEAD_DIM)
    p = jax.nn.softmax(s, axis=-1).astype(x.dtype)
    o = jnp.einsum('bhsm,bmhd->bshd', p, mv).reshape(B, S, W)
    cattn = o * jax.nn.silu(x_gate)
    br = jnp.einsum('bsnw,nwd->bsnd', jnp.stack([a, bconv, cattn], axis=2), w_branch)
    merged = jnp.sum(jax.nn.sigmoid(merge_logits) * br, axis=2)
    y = merged @ w_out
    x_new = x + rmsnorm(y, g_post)
    return x_new, pool_ext[:, -POOL_BUF:], conv_ext[:, -CONV_BUF:]


def setup_inputs(seed: int = 0) -> dict:
    key = jax.random.key(seed)
    ks = jax.random.split(key, 24)
    f = jnp.float32
    W = BRANCH_W
    nrm = lambda k, shape, s: jax.random.normal(k, shape, f) * s
    return {
        "x_prompt": nrm(ks[0], (BATCH, SEQ, D_MODEL), 1.0),
        "x_sample": nrm(ks[1], (DEC_BATCH, DEC_SEQ, D_MODEL), 1.0),
        "state_pool": nrm(ks[2], (DEPTH, DEC_BATCH, POOL_BUF, W), 1.0),
        "state_conv": nrm(ks[3], (DEPTH, DEC_BATCH, CONV_BUF, W), 0.5),
        "cache_mem_k": nrm(ks[4], (DEPTH, DEC_BATCH, N_MEM, N_XHEADS, XHEAD_DIM), 1.0),
        "cache_mem_v": nrm(ks[5], (DEPTH, DEC_BATCH, N_MEM, N_XHEADS, XHEAD_DIM), 1.0),
        "mem_prompt": nrm(ks[6], (BATCH, N_MEM, D_MODEL), 1.0),
        "norm_pre": 1.0 + nrm(ks[7], (DEPTH, D_MODEL), 0.05),
        "norm_post": 1.0 + nrm(ks[8], (DEPTH, D_MODEL), 0.05),
        "mem_norm": 1.0 + nrm(ks[9], (DEPTH, D_MODEL), 0.05),
        "w_mem_kv": nrm(ks[10], (DEPTH, D_MODEL, 2 * W), D_MODEL ** -0.5),
        "w_in": nrm(ks[11], (DEPTH, D_MODEL, IN_COLS), D_MODEL ** -0.5),
        "pool_w": nrm(ks[12], (DEPTH, N_POOL_GROUPS, POOL_GROUP_W, POOL_GROUP_W), POOL_GROUP_W ** -0.5),
        "pool_scale": 1.0 + nrm(ks[13], (DEPTH, W), 0.1),
        "conv_w": nrm(ks[14], (DEPTH, CONV_WIDTH, W), CONV_WIDTH ** -0.5),
        "conv_b": nrm(ks[15], (DEPTH, W), 0.02),
        "conv_ln_g": 1.0 + nrm(ks[16], (DEPTH, W), 0.05),
        "conv_ln_b": nrm(ks[17], (DEPTH, W), 0.02),
        "w_branch": nrm(ks[18], (DEPTH, N_BRANCH, W, D_MODEL), W ** -0.5),
        "w_out": nrm(ks[19], (DEPTH, D_MODEL, D_MODEL), D_MODEL ** -0.5),
    }


def reference(x_prompt, x_sample, state_pool, state_conv, cache_mem_k, cache_mem_v, mem_prompt,
              norm_pre, norm_post, mem_norm, w_mem_kv, w_in, pool_w, pool_scale, conv_w, conv_b,
              conv_ln_g, conv_ln_b, w_branch, w_out):
    xp = x_prompt
    xs = x_sample
    pool_p, conv_p, mk_p, mv_p, pool_s, conv_s = [], [], [], [], [], []
    for i in range(DEPTH):
        lw = (norm_pre[i], norm_post[i], w_in[i], pool_w[i], pool_scale[i], conv_w[i], conv_b[i],
              conv_ln_g[i], conv_ln_b[i], w_branch[i], w_out[i])
        mk, mv = mem_kv(mem_prompt, mem_norm[i], w_mem_kv[i])
        zp = jnp.zeros((xp.shape[0], POOL_BUF, BRANCH_W), xp.dtype)
        zc = jnp.zeros((xp.shape[0], CONV_BUF, BRANCH_W), xp.dtype)
        xp, nb_pool, nb_conv = layer(xp, zp, zc, mk, mv, 0, *lw)
        pool_p.append(nb_pool)
        conv_p.append(nb_conv)
        mk_p.append(mk)
        mv_p.append(mv)
        xs, ns_pool, ns_conv = layer(xs, state_pool[i], state_conv[i], cache_mem_k[i], cache_mem_v[i],
                                     PAST_LEN, *lw)
        pool_s.append(ns_pool)
        conv_s.append(ns_conv)
    return (xp, xs, jnp.stack(pool_p), jnp.stack(conv_p), jnp.stack(mk_p), jnp.stack(mv_p),
            jnp.stack(pool_s), jnp.stack(conv_s))
```

```python
import functools
import math

import jax
import jax.numpy as jnp
from jax import lax
from jax.experimental import pallas as pl
from jax.experimental.pallas import tpu as pltpu

D_MODEL = 1024
DEPTH = 2
DEC_SEQ = 8
PAST_LEN = 16384
BRANCH_W = D_MODEL // 2
N_BRANCH = 3
POOL_WINDOWS = (2, 4, 8, 16)
POOL_GROUP_W = BRANCH_W // len(POOL_WINDOWS)
POOL_BUF = max(POOL_WINDOWS) - 1
CONV_WIDTH = 31
CONV_BUF = CONV_WIDTH - 1
N_MEM = 256
N_XHEADS = 4
XHEAD_DIM = BRANCH_W // N_XHEADS
N_IN_SLICES = 7
EPS = 1e-6

LANES = 128
SUBLANES = 8
POOL_HEAD = 16
CONV_HEAD = 32
PROMPT_TILE = 256
SAMPLE_SEQS = 8
VMEM_LIMIT = 56 * 1024 * 1024

F32 = jnp.float32
BF16 = jnp.bfloat16


def _sigmoid(x):
    return 1.0 / (1.0 + jnp.exp(-x))


def _silu(x):
    return x * _sigmoid(x)


def _rmsnorm(x, g):
    return x * lax.rsqrt(jnp.mean(x * x, axis=-1, keepdims=True) + EPS) * g


def _dot(a, b):
    return jnp.dot(a, b, preferred_element_type=F32)


def _pool_finish(acc_parts, p_in, p_gate, cnt_parts, poolw_ref, pscale_ref):
    ys = []
    for g in range(len(POOL_WINDOWS)):
        c0 = g * POOL_GROUP_W
        mixed = acc_parts[g] / cnt_parts[g] - p_in[:, c0:c0 + POOL_GROUP_W]
        ys.append(_dot(mixed.astype(BF16), poolw_ref[g]))
    y = jnp.concatenate(ys, axis=-1)
    return y * pscale_ref[...] * _silu(p_gate)


def _conv_finish(cv, c_gate, convb_ref, lng_ref, lnb_ref):
    cv = cv + convb_ref[...]
    mu = jnp.mean(cv, axis=-1, keepdims=True)
    d = cv - mu
    var = jnp.mean(d * d, axis=-1, keepdims=True)
    r = d * lax.rsqrt(var + EPS) * lng_ref[...] + lnb_ref[...]
    return _silu(r) * _silu(c_gate)


def _merge_out(x, h, branches, win_ref, wbr_ref, wout_ref, gpost_ref):
    merged = None
    base = N_IN_SLICES * BRANCH_W
    for n, br in enumerate(branches):
        proj = _dot(br.astype(BF16), wbr_ref[n])
        gate = _sigmoid(_dot(h, win_ref[:, base + n * D_MODEL:base + (n + 1) * D_MODEL]))
        term = gate * proj
        merged = term if merged is None else merged + term
    y = _dot(merged.astype(BF16), wout_ref[...])
    return x + _rmsnorm(y, gpost_ref[...])


def _prompt_kernel(x_ref, mem_ref, gpre_ref, gpost_ref, gmem_ref, wkv_ref, win_ref, poolw_ref,
                   pscale_ref, convw_ref, convb_ref, lng_ref, lnb_ref, wbr_ref, wout_ref,
                   y_ref, npool_ref, nconv_ref, mk_ref, mv_ref,
                   pool_ext, conv_ext, kt_s, v_s, *, ts):
    W = BRANCH_W
    s = pl.program_id(1)

    @pl.when(s == 0)
    def _():
        pool_ext[0:POOL_HEAD, :] = jnp.zeros((POOL_HEAD, W), F32)
        conv_ext[0:CONV_HEAD, :] = jnp.zeros((CONV_HEAD, W), F32)
        hm = _rmsnorm(mem_ref[0], gmem_ref[...]).astype(BF16)
        kv = _dot(hm, wkv_ref[...])
        k = kv[:, :W]
        v = kv[:, W:]
        mk_ref[0] = k
        mv_ref[0] = v
        kt_s[...] = k.T.astype(BF16)
        v_s[...] = v.astype(BF16)

    x = x_ref[0]
    h = _rmsnorm(x, gpre_ref[...]).astype(BF16)

    pa = _dot(h, win_ref[:, 0:2 * W])
    p_in = pa[:, :W]
    p_gate = pa[:, W:]
    pool_ext[POOL_HEAD:POOL_HEAD + ts, :] = p_in
    pos1 = (lax.broadcasted_iota(jnp.int32, (ts, POOL_GROUP_W), 0) + s * ts + 1).astype(F32)
    accs, cnts = [], []
    for g, win in enumerate(POOL_WINDOWS):
        c0 = g * POOL_GROUP_W
        acc = pool_ext[POOL_HEAD:POOL_HEAD + ts, c0:c0 + POOL_GROUP_W]
        for j in range(1, win):
            acc = acc + pool_ext[POOL_HEAD - j:POOL_HEAD - j + ts, c0:c0 + POOL_GROUP_W]
        accs.append(acc)
        cnts.append(jnp.minimum(pos1, float(win)))
    a = _pool_finish(accs, p_in, p_gate, cnts, poolw_ref, pscale_ref)
    npool_ref[0] = pool_ext[POOL_HEAD + ts - POOL_BUF:POOL_HEAD + ts, :]
    pool_ext[0:POOL_HEAD, :] = pool_ext[ts:ts + POOL_HEAD, :]

    pb = _dot(h, win_ref[:, 2 * W:5 * W])
    u = pb[:, :W] * _sigmoid(pb[:, W:2 * W])
    c_gate = pb[:, 2 * W:]
    conv_ext[CONV_HEAD:CONV_HEAD + ts, :] = u
    off = CONV_HEAD - CONV_BUF
    slabs = []
    for c in range(W // LANES):
        c0 = c * LANES
        acc = None
        for k in range(CONV_WIDTH):
            term = conv_ext[off + k:off + k + ts, c0:c0 + LANES] * convw_ref[k:k + 1, c0:c0 + LANES]
            acc = term if acc is None else acc + term
        slabs.append(acc)
    bconv = _conv_finish(jnp.concatenate(slabs, axis=-1), c_gate, convb_ref, lng_ref, lnb_ref)
    nconv_ref[0] = conv_ext[CONV_HEAD + ts - CONV_BUF:CONV_HEAD + ts, :]
    conv_ext[0:CONV_HEAD, :] = conv_ext[ts:ts + CONV_HEAD, :]

    pc = _dot(h, win_ref[:, 5 * W:7 * W])
    q = pc[:, :W]
    x_gate = pc[:, W:]
    scale = 1.0 / math.sqrt(XHEAD_DIM)
    outs = []
    for hd in range(N_XHEADS):
        c0 = hd * XHEAD_DIM
        sc = _dot(q[:, c0:c0 + XHEAD_DIM].astype(BF16), kt_s[c0:c0 + XHEAD_DIM, :]) * scale
        e = jnp.exp(sc - jnp.max(sc, axis=-1, keepdims=True))
        l = jnp.sum(e, axis=-1, keepdims=True)
        outs.append(_dot(e.astype(BF16), v_s[:, c0:c0 + XHEAD_DIM]) / l)
    cattn = jnp.concatenate(outs, axis=-1) * _silu(x_gate)

    y_ref[0] = _merge_out(x, h, (a, bconv, cattn), win_ref, wbr_ref, wout_ref, gpost_ref)


def _sample_kernel(x_ref, spool_ref, sconv_ref, k_ref, v_ref, gpre_ref, gpost_ref, win_ref,
                   poolw_ref, pscale_ref, convw_ref, convb_ref, lng_ref, lnb_ref, wbr_ref, wout_ref,
                   y_ref, npool_ref, nconv_ref, pool_ext, conv_ext, *, nseq):
    W = BRANCH_W
    T = nseq * DEC_SEQ
    x = x_ref[...]
    h = _rmsnorm(x, gpre_ref[...]).astype(BF16)

    pa = _dot(h, win_ref[:, 0:2 * W])
    p_in = pa[:, :W]
    p_gate = pa[:, W:]
    p0 = POOL_HEAD - POOL_BUF
    pool_ext[:, p0:POOL_HEAD, :] = spool_ref[...]
    pool_ext[:, POOL_HEAD:POOL_HEAD + DEC_SEQ, :] = p_in.reshape(nseq, DEC_SEQ, W)
    pos1 = (lax.broadcasted_iota(jnp.int32, (nseq, DEC_SEQ, POOL_GROUP_W), 1) + PAST_LEN + 1).astype(F32)
    pos1 = pos1.reshape(T, POOL_GROUP_W)
    accs, cnts = [], []
    for g, win in enumerate(POOL_WINDOWS):
        c0 = g * POOL_GROUP_W
        acc = pool_ext[:, POOL_HEAD:POOL_HEAD + DEC_SEQ, c0:c0 + POOL_GROUP_W]
        for j in range(1, win):
            acc = acc + pool_ext[:, POOL_HEAD - j:POOL_HEAD - j + DEC_SEQ, c0:c0 + POOL_GROUP_W]
        accs.append(acc.reshape(T, POOL_GROUP_W))
        cnts.append(jnp.minimum(pos1, float(win)))
    a = _pool_finish(accs, p_in, p_gate, cnts, poolw_ref, pscale_ref)
    npool_ref[...] = pool_ext[:, POOL_HEAD + DEC_SEQ - POOL_BUF:POOL_HEAD + DEC_SEQ, :]

    pb = _dot(h, win_ref[:, 2 * W:5 * W])
    u = pb[:, :W] * _sigmoid(pb[:, W:2 * W])
    c_gate = pb[:, 2 * W:]
    off = CONV_HEAD - CONV_BUF
    conv_ext[:, off:CONV_HEAD, :] = sconv_ref[...]
    conv_ext[:, CONV_HEAD:CONV_HEAD + DEC_SEQ, :] = u.reshape(nseq, DEC_SEQ, W)
    slabs = []
    for c in range(W // LANES):
        c0 = c * LANES
        acc = None
        for k in range(CONV_WIDTH):
            term = (conv_ext[:, off + k:off + k + DEC_SEQ, c0:c0 + LANES]
                    * convw_ref[k:k + 1, c0:c0 + LANES].reshape(1, 1, LANES))
            acc = term if acc is None else acc + term
        slabs.append(acc.reshape(T, LANES))
    bconv = _conv_finish(jnp.concatenate(slabs, axis=-1), c_gate, convb_ref, lng_ref, lnb_ref)
    nconv_ref[...] = conv_ext[:, CONV_HEAD + DEC_SEQ - CONV_BUF:CONV_HEAD + DEC_SEQ, :]

    pc = _dot(h, win_ref[:, 5 * W:7 * W])
    q = pc[:, :W].reshape(nseq, DEC_SEQ, W)
    x_gate = pc[:, W:]
    scale = 1.0 / math.sqrt(XHEAD_DIM)
    outs = []
    for hd in range(N_XHEADS):
        c0 = hd * XHEAD_DIM
        qh = q[:, :, c0:c0 + XHEAD_DIM].astype(BF16)
        kh = k_ref[:, :, c0:c0 + XHEAD_DIM].astype(BF16)
        vh = v_ref[:, :, c0:c0 + XHEAD_DIM].astype(BF16)
        sc = jnp.einsum('gtd,gmd->gtm', qh, kh, preferred_element_type=F32) * scale
        e = jnp.exp(sc - jnp.max(sc, axis=-1, keepdims=True))
        l = jnp.sum(e, axis=-1, keepdims=True)
        o = jnp.einsum('gtm,gmd->gtd', e.astype(BF16), vh, preferred_element_type=F32) / l
        outs.append(o.reshape(T, XHEAD_DIM))
    cattn = jnp.concatenate(outs, axis=-1) * _silu(x_gate)

    y_ref[...] = _merge_out(x, h, (a, bconv, cattn), win_ref, wbr_ref, wout_ref, gpost_ref)


def _const_spec(shape):
    zeros = (0,) * len(shape)
    return pl.BlockSpec(shape, lambda *_: zeros, pipeline_mode=pl.Buffered(1))


def _layer_spec(shape, layer):
    zeros = (0,) * len(shape)
    return pl.BlockSpec((None,) + shape, lambda *_: (layer,) + zeros, pipeline_mode=pl.Buffered(1))


def _prompt_layer(x, mem, layer, gpre, gpost, gmem, wkv, win, poolw, pscale, convw, convb, lng, lnb,
                  wbr, wout):
    B, S, D = x.shape
    W = BRANCH_W
    ts = PROMPT_TILE
    grid = (B, S // ts)
    small = lambda width: _layer_spec((1, width), layer)
    in_specs = [
        pl.BlockSpec((1, ts, D), lambda b, s: (b, s, 0)),
        pl.BlockSpec((1, N_MEM, D), lambda b, s: (b, 0, 0)),
        small(D), small(D), small(D),
        _layer_spec((D, 2 * W), layer),
        _layer_spec(win.shape[1:], layer),
        _layer_spec(poolw.shape[1:], layer),
        small(W),
        _layer_spec((CONV_WIDTH, W), layer),
        small(W), small(W), small(W),
        _layer_spec(wbr.shape[1:], layer),
        _layer_spec((D, D), layer),
    ]
    out_specs = [
        pl.BlockSpec((1, ts, D), lambda b, s: (b, s, 0)),
        pl.BlockSpec((1, POOL_BUF, W), lambda b, s: (b, 0, 0)),
        pl.BlockSpec((1, CONV_BUF, W), lambda b, s: (b, 0, 0)),
        pl.BlockSpec((1, N_MEM, W), lambda b, s: (b, 0, 0)),
        pl.BlockSpec((1, N_MEM, W), lambda b, s: (b, 0, 0)),
    ]
    out_shape = [
        jax.ShapeDtypeStruct((B, S, D), F32),
        jax.ShapeDtypeStruct((B, POOL_BUF, W), F32),
        jax.ShapeDtypeStruct((B, CONV_BUF, W), F32),
        jax.ShapeDtypeStruct((B, N_MEM, W), F32),
        jax.ShapeDtypeStruct((B, N_MEM, W), F32),
    ]
    scratch = [
        pltpu.VMEM((POOL_HEAD + ts, W), F32),
        pltpu.VMEM((CONV_HEAD + ts, W), F32),
        pltpu.VMEM((W, N_MEM), BF16),
        pltpu.VMEM((N_MEM, W), BF16),
    ]
    return pl.pallas_call(
        functools.partial(_prompt_kernel, ts=ts),
        grid=grid, in_specs=in_specs, out_specs=out_specs, out_shape=out_shape,
        scratch_shapes=scratch, name=f"prompt_layer{layer}",
        compiler_params=pltpu.CompilerParams(
            dimension_semantics=("arbitrary", "arbitrary"), vmem_limit_bytes=VMEM_LIMIT),
    )(x, mem, gpre, gpost, gmem, wkv, win, poolw, pscale, convw, convb, lng, lnb, wbr, wout)


def _sample_layer(x, spool, sconv, ck, cv, layer, gpre, gpost, win, poolw, pscale, convw, convb, lng,
                  lnb, wbr, wout):
    N, D = x.shape
    W = BRANCH_W
    nseq = SAMPLE_SEQS
    T = nseq * DEC_SEQ
    grid = (N // T,)
    small = lambda width: _layer_spec((1, width), layer)
    per_seq = lambda rows: pl.BlockSpec((None, nseq, rows, W), lambda g: (layer, g, 0, 0))
    in_specs = [
        pl.BlockSpec((T, D), lambda g: (g, 0)),
        per_seq(POOL_BUF), per_seq(CONV_BUF), per_seq(N_MEM), per_seq(N_MEM),
        small(D), small(D),
        _layer_spec(win.shape[1:], layer),
        _layer_spec(poolw.shape[1:], layer),
        small(W),
        _layer_spec((CONV_WIDTH, W), layer),
        small(W), small(W), small(W),
        _layer_spec(wbr.shape[1:], layer),
        _layer_spec((D, D), layer),
    ]
    out_specs = [
        pl.BlockSpec((T, D), lambda g: (g, 0)),
        pl.BlockSpec((nseq, POOL_BUF, W), lambda g: (g, 0, 0)),
        pl.BlockSpec((nseq, CONV_BUF, W), lambda g: (g, 0, 0)),
    ]
    nb = N // DEC_SEQ
    out_shape = [
        jax.ShapeDtypeStruct((N, D), F32),
        jax.ShapeDtypeStruct((nb, POOL_BUF, W), F32),
        jax.ShapeDtypeStruct((nb, CONV_BUF, W), F32),
    ]
    scratch = [
        pltpu.VMEM((nseq, POOL_HEAD + DEC_SEQ, W), F32),
        pltpu.VMEM((nseq, CONV_HEAD + DEC_SEQ, W), F32),
    ]
    return pl.pallas_call(
        functools.partial(_sample_kernel, nseq=nseq),
        grid=grid, in_specs=in_specs, out_specs=out_specs, out_shape=out_shape,
        scratch_shapes=scratch, name=f"sample_layer{layer}",
        compiler_params=pltpu.CompilerParams(
            dimension_semantics=("arbitrary",), vmem_limit_bytes=VMEM_LIMIT),
    )(x, spool, sconv, ck, cv, gpre, gpost, win, poolw, pscale, convw, convb, lng, lnb, wbr, wout)


def kernel(x_prompt, x_sample, state_pool, state_conv, cache_mem_k, cache_mem_v, mem_prompt,
           norm_pre, norm_post, mem_norm, w_mem_kv, w_in, pool_w, pool_scale, conv_w, conv_b,
           conv_ln_g, conv_ln_b, w_branch, w_out):
    W = BRANCH_W
    nb, nt, D = x_sample.shape
    row = lambda p: p.reshape(DEPTH, 1, p.shape[-1])
    gpre, gpost, gmem = row(norm_pre), row(norm_post), row(mem_norm)
    pscale, convb, lng, lnb = row(pool_scale), row(conv_b), row(conv_ln_g), row(conv_ln_b)
    wkv, win, poolw = w_mem_kv.astype(BF16), w_in.astype(BF16), pool_w.astype(BF16)
    wbr, wout = w_branch.astype(BF16), w_out.astype(BF16)
    ck = cache_mem_k.reshape(DEPTH, nb, N_MEM, W)
    cv = cache_mem_v.reshape(DEPTH, nb, N_MEM, W)

    xp = x_prompt
    xs = x_sample.reshape(nb * nt, D)
    pool_p, conv_p, mk_p, mv_p, pool_s, conv_s = [], [], [], [], [], []
    for i in range(DEPTH):
        xp, npool, nconv, mk, mv = _prompt_layer(
            xp, mem_prompt, i, gpre, gpost, gmem, wkv, win, poolw, pscale, conv_w, convb, lng, lnb,
            wbr, wout)
        pool_p.append(npool)
        conv_p.append(nconv)
        mk_p.append(mk.reshape(mk.shape[0], N_MEM, N_XHEADS, XHEAD_DIM))
        mv_p.append(mv.reshape(mv.shape[0], N_MEM, N_XHEADS, XHEAD_DIM))
        xs, nspool, nsconv = _sample_layer(
            xs, state_pool, state_conv, ck, cv, i, gpre, gpost, win, poolw, pscale, conv_w, convb,
            lng, lnb, wbr, wout)
        pool_s.append(nspool)
        conv_s.append(nsconv)
    return (xp, xs.reshape(nb, nt, D), jnp.stack(pool_p), jnp.stack(conv_p), jnp.stack(mk_p),
            jnp.stack(mv_p), jnp.stack(pool_s), jnp.stack(conv_s))
```

```python
import functools
import math

import jax
import jax.numpy as jnp
from jax import lax
from jax.experimental import pallas as pl
from jax.experimental.pallas import tpu as pltpu

D_MODEL = 1024
DEPTH = 2
DEC_SEQ = 8
PAST_LEN = 16384
BRANCH_W = D_MODEL // 2
N_BRANCH = 3
POOL_WINDOWS = (2, 4, 8, 16)
POOL_GROUP_W = BRANCH_W // len(POOL_WINDOWS)
POOL_BUF = max(POOL_WINDOWS) - 1
CONV_WIDTH = 31
CONV_BUF = CONV_WIDTH - 1
N_MEM = 256
N_XHEADS = 4
XHEAD_DIM = BRANCH_W // N_XHEADS
N_IN_SLICES = 7
EPS = 1e-6

LANES = 128
SUBLANES = 8
POOL_HEAD = 16
CONV_HEAD = 32
PROMPT_TILE = 256
SAMPLE_SEQS = 8
VMEM_LIMIT = 56 * 1024 * 1024
NEG_LOG2E = -1.0 / math.log(2.0)
MASKED_SCORE = -1e30

F32 = jnp.float32
BF16 = jnp.bfloat16


def _sigmoid(x):
    return 1.0 / (1.0 + jnp.exp2(x * NEG_LOG2E))


def _silu(x):
    return x * _sigmoid(x)


def _rmsnorm(x, g):
    return x * lax.rsqrt(jnp.mean(x * x, axis=-1, keepdims=True) + EPS) * g


def _dot(a, b):
    return jnp.dot(a, b, preferred_element_type=F32)


def _pool_finish(acc_parts, p_in, p_gate, cnt_parts, poolw_ref, pscale_ref):
    ys = []
    for g in range(len(POOL_WINDOWS)):
        c0 = g * POOL_GROUP_W
        mixed = acc_parts[g] / cnt_parts[g] - p_in[:, c0:c0 + POOL_GROUP_W]
        ys.append(_dot(mixed.astype(BF16), poolw_ref[g]))
    y = jnp.concatenate(ys, axis=-1)
    return y * pscale_ref[...] * _silu(p_gate)


def _conv_finish(cv, c_gate, convb_ref, lng_ref, lnb_ref):
    cv = cv + convb_ref[...]
    mu = jnp.mean(cv, axis=-1, keepdims=True)
    d = cv - mu
    var = jnp.mean(d * d, axis=-1, keepdims=True)
    r = d * lax.rsqrt(var + EPS) * lng_ref[...] + lnb_ref[...]
    return _silu(r) * _silu(c_gate)


def _merge_out(x, h, branches, win_ref, wbr_ref, wout_ref, gpost_ref):
    merged = None
    base = N_IN_SLICES * BRANCH_W
    for n, br in enumerate(branches):
        proj = _dot(br.astype(BF16), wbr_ref[n])
        gate = _sigmoid(_dot(h, win_ref[:, base + n * D_MODEL:base + (n + 1) * D_MODEL]))
        term = gate * proj
        merged = term if merged is None else merged + term
    y = _dot(merged.astype(BF16), wout_ref[...])
    return x + _rmsnorm(y, gpost_ref[...])


def _prompt_kernel(x_ref, mem_ref, gpre_ref, gpost_ref, gmem_ref, wkv_ref, win_ref, poolw_ref,
                   pscale_ref, convw_ref, convb_ref, lng_ref, lnb_ref, wbr_ref, wout_ref,
                   y_ref, npool_ref, nconv_ref, mk_ref, mv_ref,
                   pool_ext, conv_ext, conv_sh, gate_s, kt_s, v_s, *, ts):
    W = BRANCH_W
    s = pl.program_id(1)

    @pl.when(s == 0)
    def _():
        pool_ext[0:POOL_HEAD, :] = jnp.zeros((POOL_HEAD, W), F32)
        conv_ext[0:CONV_HEAD, :] = jnp.zeros((CONV_HEAD, W), F32)
        hm = _rmsnorm(mem_ref[0], gmem_ref[...]).astype(BF16)
        kv = _dot(hm, wkv_ref[...])
        k = kv[:, :W]
        v = kv[:, W:]
        mk_ref[0] = k
        mv_ref[0] = v
        kt_s[...] = k.T.astype(BF16)
        v_s[...] = v.astype(BF16)

    x = x_ref[0]
    h = _rmsnorm(x, gpre_ref[...]).astype(BF16)

    pb = _dot(h, win_ref[:, 2 * W:5 * W])
    u = pb[:, :W] * _sigmoid(pb[:, W:2 * W])
    c_gate = pb[:, 2 * W:]
    conv_ext[CONV_HEAD:CONV_HEAD + ts, :] = u
    for r in range(1, SUBLANES):
        conv_sh[r - 1, :, :] = conv_ext[r:r + ts + CONV_HEAD - SUBLANES, :]

    off = CONV_HEAD - CONV_BUF
    gate_base = N_IN_SLICES * W
    n_slabs = W // LANES
    gate_chunk = N_BRANCH * D_MODEL // n_slabs
    slabs = []
    for c in range(n_slabs):
        g0 = c * gate_chunk
        gate_s[:, g0:g0 + gate_chunk] = _sigmoid(
            _dot(h, win_ref[:, gate_base + g0:gate_base + g0 + gate_chunk]))
        c0 = c * LANES
        acc = None
        for k in range(CONV_WIDTH):
            a8, r = divmod(off + k, SUBLANES)
            if r == 0:
                rows = conv_ext[a8 * SUBLANES:a8 * SUBLANES + ts, c0:c0 + LANES]
            else:
                rows = conv_sh[r - 1, a8 * SUBLANES:a8 * SUBLANES + ts, c0:c0 + LANES]
            term = rows * convw_ref[k:k + 1, c0:c0 + LANES]
            acc = term if acc is None else acc + term
        slabs.append(acc)
    bconv = _conv_finish(jnp.concatenate(slabs, axis=-1), c_gate, convb_ref, lng_ref, lnb_ref)
    nconv_ref[0] = conv_ext[CONV_HEAD + ts - CONV_BUF:CONV_HEAD + ts, :]
    conv_ext[0:CONV_HEAD, :] = conv_ext[ts:ts + CONV_HEAD, :]

    pa = _dot(h, win_ref[:, 0:2 * W])
    p_in = pa[:, :W]
    p_gate = pa[:, W:]
    pool_ext[POOL_HEAD:POOL_HEAD + ts, :] = p_in
    pos1 = (lax.broadcasted_iota(jnp.int32, (ts, POOL_GROUP_W), 0) + s * ts + 1).astype(F32)
    accs, cnts = [], []
    for g, win in enumerate(POOL_WINDOWS):
        c0 = g * POOL_GROUP_W
        acc = pool_ext[POOL_HEAD:POOL_HEAD + ts, c0:c0 + POOL_GROUP_W]
        for j in range(1, win):
            acc = acc + pool_ext[POOL_HEAD - j:POOL_HEAD - j + ts, c0:c0 + POOL_GROUP_W]
        accs.append(acc)
        cnts.append(jnp.minimum(pos1, float(win)))
    a = _pool_finish(accs, p_in, p_gate, cnts, poolw_ref, pscale_ref)
    npool_ref[0] = pool_ext[POOL_HEAD + ts - POOL_BUF:POOL_HEAD + ts, :]
    pool_ext[0:POOL_HEAD, :] = pool_ext[ts:ts + POOL_HEAD, :]

    pc = _dot(h, win_ref[:, 5 * W:7 * W])
    q = pc[:, :W]
    x_gate = pc[:, W:]
    scale = 1.0 / math.sqrt(XHEAD_DIM)
    outs = []
    for hd in range(N_XHEADS):
        c0 = hd * XHEAD_DIM
        sc = _dot(q[:, c0:c0 + XHEAD_DIM].astype(BF16), kt_s[c0:c0 + XHEAD_DIM, :]) * scale
        e = jnp.exp(sc - jnp.max(sc, axis=-1, keepdims=True))
        l = jnp.sum(e, axis=-1, keepdims=True)
        outs.append(_dot(e.astype(BF16), v_s[:, c0:c0 + XHEAD_DIM]) / l)
    cattn = jnp.concatenate(outs, axis=-1) * _silu(x_gate)

    merged = None
    for n, br in enumerate((a, bconv, cattn)):
        term = gate_s[:, n * D_MODEL:(n + 1) * D_MODEL] * _dot(br.astype(BF16), wbr_ref[n])
        merged = term if merged is None else merged + term
    y = _dot(merged.astype(BF16), wout_ref[...])
    y_ref[0] = x + _rmsnorm(y, gpost_ref[...])


def _sample_kernel(x_ref, spool_ref, sconv_ref, k_ref, v_ref, gpre_ref, gpost_ref, win_ref,
                   poolw_ref, pscale_ref, convw_ref, convb_ref, lng_ref, lnb_ref, wbr_ref, wout_ref,
                   y_ref, npool_ref, nconv_ref, pool_ext, conv_ext, *, nseq):
    W = BRANCH_W
    T = nseq * DEC_SEQ
    x = x_ref[...]
    h = _rmsnorm(x, gpre_ref[...]).astype(BF16)

    pa = _dot(h, win_ref[:, 0:2 * W])
    p_in = pa[:, :W]
    p_gate = pa[:, W:]
    p0 = POOL_HEAD - POOL_BUF
    pool_ext[:, p0:POOL_HEAD, :] = spool_ref[...]
    pool_ext[:, POOL_HEAD:POOL_HEAD + DEC_SEQ, :] = p_in.reshape(nseq, DEC_SEQ, W)
    pos1 = (lax.broadcasted_iota(jnp.int32, (nseq, DEC_SEQ, POOL_GROUP_W), 1) + PAST_LEN + 1).astype(F32)
    pos1 = pos1.reshape(T, POOL_GROUP_W)
    accs, cnts = [], []
    for g, win in enumerate(POOL_WINDOWS):
        c0 = g * POOL_GROUP_W
        acc = pool_ext[:, POOL_HEAD:POOL_HEAD + DEC_SEQ, c0:c0 + POOL_GROUP_W]
        for j in range(1, win):
            acc = acc + pool_ext[:, POOL_HEAD - j:POOL_HEAD - j + DEC_SEQ, c0:c0 + POOL_GROUP_W]
        accs.append(acc.reshape(T, POOL_GROUP_W))
        cnts.append(jnp.minimum(pos1, float(win)))
    a = _pool_finish(accs, p_in, p_gate, cnts, poolw_ref, pscale_ref)
    npool_ref[...] = pool_ext[:, POOL_HEAD + DEC_SEQ - POOL_BUF:POOL_HEAD + DEC_SEQ, :]

    pb = _dot(h, win_ref[:, 2 * W:5 * W])
    u = pb[:, :W] * _sigmoid(pb[:, W:2 * W])
    c_gate = pb[:, 2 * W:]
    off = CONV_HEAD - CONV_BUF
    conv_ext[:, off:CONV_HEAD, :] = sconv_ref[...]
    conv_ext[:, CONV_HEAD:CONV_HEAD + DEC_SEQ, :] = u.reshape(nseq, DEC_SEQ, W)
    slabs = []
    for c in range(W // LANES):
        c0 = c * LANES
        acc = None
        for k in range(CONV_WIDTH):
            term = (conv_ext[:, off + k:off + k + DEC_SEQ, c0:c0 + LANES]
                    * convw_ref[k:k + 1, c0:c0 + LANES].reshape(1, 1, LANES))
            acc = term if acc is None else acc + term
        slabs.append(acc.reshape(T, LANES))
    bconv = _conv_finish(jnp.concatenate(slabs, axis=-1), c_gate, convb_ref, lng_ref, lnb_ref)
    nconv_ref[...] = conv_ext[:, CONV_HEAD + DEC_SEQ - CONV_BUF:CONV_HEAD + DEC_SEQ, :]

    pc = _dot(h, win_ref[:, 5 * W:7 * W])
    q = pc[:, :W].reshape(nseq, DEC_SEQ, W)
    x_gate = pc[:, W:]
    scale = 1.0 / math.sqrt(XHEAD_DIM)
    nq = N_XHEADS * DEC_SEQ
    nk = N_MEM * N_XHEADS
    q4 = jnp.concatenate(
        [q[:, :, hd * XHEAD_DIM:(hd + 1) * XHEAD_DIM] for hd in range(N_XHEADS)], axis=1).astype(BF16)
    sc = jnp.einsum('gqd,gkd->gqk', q4, k_ref[...].astype(BF16), preferred_element_type=F32) * scale
    q_head = lax.broadcasted_iota(jnp.int32, (nq, nk), 0) // DEC_SEQ
    k_head = lax.broadcasted_iota(jnp.int32, (nq, nk), 1) % N_XHEADS
    sc = jnp.where((q_head == k_head)[None], sc, MASKED_SCORE)
    e = jnp.exp(sc - jnp.max(sc, axis=-1, keepdims=True))
    l = jnp.sum(e, axis=-1, keepdims=True)
    o = jnp.einsum('gqk,gkd->gqd', e.astype(BF16), v_ref[...].astype(BF16),
                   preferred_element_type=F32) / l
    cattn = jnp.concatenate(
        [o[:, hd * DEC_SEQ:(hd + 1) * DEC_SEQ, :].reshape(T, XHEAD_DIM) for hd in range(N_XHEADS)],
        axis=-1) * _silu(x_gate)

    y_ref[...] = _merge_out(x, h, (a, bconv, cattn), win_ref, wbr_ref, wout_ref, gpost_ref)


def _const_spec(shape):
    zeros = (0,) * len(shape)
    return pl.BlockSpec(shape, lambda *_: zeros, pipeline_mode=pl.Buffered(1))


def _layer_spec(shape, layer):
    zeros = (0,) * len(shape)
    return pl.BlockSpec((None,) + shape, lambda *_: (layer,) + zeros, pipeline_mode=pl.Buffered(1))


def _prompt_layer(x, mem, layer, gpre, gpost, gmem, wkv, win, poolw, pscale, convw, convb, lng, lnb,
                  wbr, wout):
    B, S, D = x.shape
    W = BRANCH_W
    ts = PROMPT_TILE
    grid = (B, S // ts)
    small = lambda width: _layer_spec((1, width), layer)
    in_specs = [
        pl.BlockSpec((1, ts, D), lambda b, s: (b, s, 0)),
        pl.BlockSpec((1, N_MEM, D), lambda b, s: (b, 0, 0)),
        small(D), small(D), small(D),
        _layer_spec((D, 2 * W), layer),
        _layer_spec(win.shape[1:], layer),
        _layer_spec(poolw.shape[1:], layer),
        small(W),
        _layer_spec((CONV_WIDTH, W), layer),
        small(W), small(W), small(W),
        _layer_spec(wbr.shape[1:], layer),
        _layer_spec((D, D), layer),
    ]
    out_specs = [
        pl.BlockSpec((1, ts, D), lambda b, s: (b, s, 0)),
        pl.BlockSpec((1, POOL_BUF, W), lambda b, s: (b, 0, 0)),
        pl.BlockSpec((1, CONV_BUF, W), lambda b, s: (b, 0, 0)),
        pl.BlockSpec((1, N_MEM, W), lambda b, s: (b, 0, 0)),
        pl.BlockSpec((1, N_MEM, W), lambda b, s: (b, 0, 0)),
    ]
    out_shape = [
        jax.ShapeDtypeStruct((B, S, D), F32),
        jax.ShapeDtypeStruct((B, POOL_BUF, W), F32),
        jax.ShapeDtypeStruct((B, CONV_BUF, W), F32),
        jax.ShapeDtypeStruct((B, N_MEM, W), F32),
        jax.ShapeDtypeStruct((B, N_MEM, W), F32),
    ]
    scratch = [
        pltpu.VMEM((POOL_HEAD + ts, W), F32),
        pltpu.VMEM((CONV_HEAD + ts, W), F32),
        pltpu.VMEM((SUBLANES - 1, ts + CONV_HEAD - SUBLANES, W), F32),
        pltpu.VMEM((ts, N_BRANCH * D_MODEL), F32),
        pltpu.VMEM((W, N_MEM), BF16),
        pltpu.VMEM((N_MEM, W), BF16),
    ]
    return pl.pallas_call(
        functools.partial(_prompt_kernel, ts=ts),
        grid=grid, in_specs=in_specs, out_specs=out_specs, out_shape=out_shape,
        scratch_shapes=scratch, name=f"prompt_layer{layer}",
        compiler_params=pltpu.CompilerParams(
            dimension_semantics=("arbitrary", "arbitrary"), vmem_limit_bytes=VMEM_LIMIT),
    )(x, mem, gpre, gpost, gmem, wkv, win, poolw, pscale, convw, convb, lng, lnb, wbr, wout)


def _sample_layer(x, spool, sconv, ck, cv, layer, gpre, gpost, win, poolw, pscale, convw, convb, lng,
                  lnb, wbr, wout):
    N, D = x.shape
    W = BRANCH_W
    nseq = SAMPLE_SEQS
    T = nseq * DEC_SEQ
    grid = (N // T,)
    small = lambda width: _layer_spec((1, width), layer)
    per_seq = lambda rows: pl.BlockSpec((None, nseq, rows, W), lambda g: (layer, g, 0, 0))
    cache = pl.BlockSpec((None, nseq, N_MEM * N_XHEADS, XHEAD_DIM), lambda g: (layer, g, 0, 0))
    in_specs = [
        pl.BlockSpec((T, D), lambda g: (g, 0)),
        per_seq(POOL_BUF), per_seq(CONV_BUF), cache, cache,
        small(D), small(D),
        _layer_spec(win.shape[1:], layer),
        _layer_spec(poolw.shape[1:], layer),
        small(W),
        _layer_spec((CONV_WIDTH, W), layer),
        small(W), small(W), small(W),
        _layer_spec(wbr.shape[1:], layer),
        _layer_spec((D, D), layer),
    ]
    out_specs = [
        pl.BlockSpec((T, D), lambda g: (g, 0)),
        pl.BlockSpec((nseq, POOL_BUF, W), lambda g: (g, 0, 0)),
        pl.BlockSpec((nseq, CONV_BUF, W), lambda g: (g, 0, 0)),
    ]
    nb = N // DEC_SEQ
    out_shape = [
        jax.ShapeDtypeStruct((N, D), F32),
        jax.ShapeDtypeStruct((nb, POOL_BUF, W), F32),
        jax.ShapeDtypeStruct((nb, CONV_BUF, W), F32),
    ]
    scratch = [
        pltpu.VMEM((nseq, POOL_HEAD + DEC_SEQ, W), F32),
        pltpu.VMEM((nseq, CONV_HEAD + DEC_SEQ, W), F32),
    ]
    return pl.pallas_call(
        functools.partial(_sample_kernel, nseq=nseq),
        grid=grid, in_specs=in_specs, out_specs=out_specs, out_shape=out_shape,
        scratch_shapes=scratch, name=f"sample_layer{layer}",
        compiler_params=pltpu.CompilerParams(
            dimension_semantics=("arbitrary",), vmem_limit_bytes=VMEM_LIMIT),
    )(x, spool, sconv, ck, cv, gpre, gpost, win, poolw, pscale, convw, convb, lng, lnb, wbr, wout)


def kernel(x_prompt, x_sample, state_pool, state_conv, cache_mem_k, cache_mem_v, mem_prompt,
           norm_pre, norm_post, mem_norm, w_mem_kv, w_in, pool_w, pool_scale, conv_w, conv_b,
           conv_ln_g, conv_ln_b, w_branch, w_out):
    W = BRANCH_W
    nb, nt, D = x_sample.shape
    row = lambda p: p.reshape(DEPTH, 1, p.shape[-1])
    gpre, gpost, gmem = row(norm_pre), row(norm_post), row(mem_norm)
    pscale, convb, lng, lnb = row(pool_scale), row(conv_b), row(conv_ln_g), row(conv_ln_b)
    wkv, win, poolw = w_mem_kv.astype(BF16), w_in.astype(BF16), pool_w.astype(BF16)
    wbr, wout = w_branch.astype(BF16), w_out.astype(BF16)
    ck = cache_mem_k.reshape(DEPTH, nb, N_MEM * N_XHEADS, XHEAD_DIM)
    cv = cache_mem_v.reshape(DEPTH, nb, N_MEM * N_XHEADS, XHEAD_DIM)

    xp = x_prompt
    xs = x_sample.reshape(nb * nt, D)
    pool_p, conv_p, mk_p, mv_p, pool_s, conv_s = [], [], [], [], [], []
    for i in range(DEPTH):
        xp, npool, nconv, mk, mv = _prompt_layer(
            xp, mem_prompt, i, gpre, gpost, gmem, wkv, win, poolw, pscale, conv_w, convb, lng, lnb,
            wbr, wout)
        pool_p.append(npool)
        conv_p.append(nconv)
        mk_p.append(mk.reshape(mk.shape[0], N_MEM, N_XHEADS, XHEAD_DIM))
        mv_p.append(mv.reshape(mv.shape[0], N_MEM, N_XHEADS, XHEAD_DIM))
        xs, nspool, nsconv = _sample_layer(
            xs, state_pool, state_conv, ck, cv, i, gpre, gpost, win, poolw, pscale, conv_w, convb,
            lng, lnb, wbr, wout)
        pool_s.append(nspool)
        conv_s.append(nsconv)
    return (xp, xs.reshape(nb, nt, D), jnp.stack(pool_p), jnp.stack(conv_p), jnp.stack(mk_p),
            jnp.stack(mv_p), jnp.stack(pool_s), jnp.stack(conv_s))
```

```python
import functools
import math

import jax
import jax.numpy as jnp
from jax import lax
from jax.experimental import pallas as pl
from jax.experimental.pallas import tpu as pltpu

D_MODEL = 1024
DEPTH = 2
DEC_SEQ = 8
PAST_LEN = 16384
BRANCH_W = D_MODEL // 2
N_BRANCH = 3
POOL_WINDOWS = (2, 4, 8, 16)
POOL_GROUP_W = BRANCH_W // len(POOL_WINDOWS)
POOL_BUF = max(POOL_WINDOWS) - 1
CONV_WIDTH = 31
CONV_BUF = CONV_WIDTH - 1
N_MEM = 256
N_XHEADS = 4
XHEAD_DIM = BRANCH_W // N_XHEADS
N_IN_SLICES = 7
EPS = 1e-6

LANES = 128
SUBLANES = 8
POOL_HEAD = 16
CONV_HEAD = 32
PROMPT_TILE = 256
SAMPLE_SEQS = 16
VMEM_LIMIT = 60 * 1024 * 1024
NEG_LOG2E = -1.0 / math.log(2.0)
MASKED_SCORE = -1e30

F32 = jnp.float32
BF16 = jnp.bfloat16


def _sigmoid(x):
    return 1.0 / (1.0 + jnp.exp2(x * NEG_LOG2E))


def _silu(x):
    return x * _sigmoid(x)


def _rmsnorm(x, g):
    return x * lax.rsqrt(jnp.mean(x * x, axis=-1, keepdims=True) + EPS) * g


def _dot(a, b):
    return jnp.dot(a, b, preferred_element_type=F32)


def _pool_finish(acc_parts, p_in, p_gate, cnt_parts, poolw_ref, pscale_ref):
    ys = []
    for g in range(len(POOL_WINDOWS)):
        c0 = g * POOL_GROUP_W
        mixed = acc_parts[g] / cnt_parts[g] - p_in[:, c0:c0 + POOL_GROUP_W]
        ys.append(_dot(mixed.astype(BF16), poolw_ref[g]))
    y = jnp.concatenate(ys, axis=-1)
    return y * pscale_ref[...] * _silu(p_gate)


def _conv_finish(cv, c_gate, convb_ref, lng_ref, lnb_ref):
    cv = cv + convb_ref[...]
    mu = jnp.mean(cv, axis=-1, keepdims=True)
    d = cv - mu
    var = jnp.mean(d * d, axis=-1, keepdims=True)
    r = d * lax.rsqrt(var + EPS) * lng_ref[...] + lnb_ref[...]
    return _silu(r) * _silu(c_gate)


def _merge_out(x, h, branches, win_ref, wbr_ref, wout_ref, gpost_ref):
    merged = None
    base = N_IN_SLICES * BRANCH_W
    for n, br in enumerate(branches):
        proj = _dot(br.astype(BF16), wbr_ref[n])
        gate = _sigmoid(_dot(h, win_ref[:, base + n * D_MODEL:base + (n + 1) * D_MODEL]))
        term = gate * proj
        merged = term if merged is None else merged + term
    y = _dot(merged.astype(BF16), wout_ref[...])
    return x + _rmsnorm(y, gpost_ref[...])


def _prompt_kernel(x_ref, mem_ref, gpre_ref, gpost_ref, gmem_ref, wkv_ref, win_ref, poolw_ref,
                   pscale_ref, convw_ref, convb_ref, lng_ref, lnb_ref, wbr_ref, wout_ref,
                   y_ref, npool_ref, nconv_ref, mk_ref, mv_ref,
                   pool_ext, conv_ext, conv_sh, gate_s, kt_s, v_s, *, ts):
    W = BRANCH_W
    s = pl.program_id(1)

    @pl.when(s == 0)
    def _():
        pool_ext[0:POOL_HEAD, :] = jnp.zeros((POOL_HEAD, W), F32)
        conv_ext[0:CONV_HEAD, :] = jnp.zeros((CONV_HEAD, W), F32)
        hm = _rmsnorm(mem_ref[0], gmem_ref[...]).astype(BF16)
        kv = _dot(hm, wkv_ref[...])
        k = kv[:, :W]
        v = kv[:, W:]
        mk_ref[0] = k
        mv_ref[0] = v
        kt_s[...] = k.T.astype(BF16)
        v_s[...] = v.astype(BF16)

    x = x_ref[0]
    h = _rmsnorm(x, gpre_ref[...]).astype(BF16)

    pb = _dot(h, win_ref[:, 2 * W:5 * W])
    u = pb[:, :W] * _sigmoid(pb[:, W:2 * W])
    c_gate = pb[:, 2 * W:]
    conv_ext[CONV_HEAD:CONV_HEAD + ts, :] = u
    for r in range(1, SUBLANES):
        conv_sh[r - 1, :, :] = conv_ext[r:r + ts + CONV_HEAD - SUBLANES, :]

    off = CONV_HEAD - CONV_BUF
    gate_base = N_IN_SLICES * W
    n_slabs = W // LANES
    gate_chunk = N_BRANCH * D_MODEL // n_slabs
    slabs = []
    for c in range(n_slabs):
        g0 = c * gate_chunk
        gate_s[:, g0:g0 + gate_chunk] = _sigmoid(
            _dot(h, win_ref[:, gate_base + g0:gate_base + g0 + gate_chunk]))
        c0 = c * LANES
        acc = None
        for k in range(CONV_WIDTH):
            a8, r = divmod(off + k, SUBLANES)
            if r == 0:
                rows = conv_ext[a8 * SUBLANES:a8 * SUBLANES + ts, c0:c0 + LANES]
            else:
                rows = conv_sh[r - 1, a8 * SUBLANES:a8 * SUBLANES + ts, c0:c0 + LANES]
            term = rows * convw_ref[k:k + 1, c0:c0 + LANES]
            acc = term if acc is None else acc + term
        slabs.append(acc)
    bconv = _conv_finish(jnp.concatenate(slabs, axis=-1), c_gate, convb_ref, lng_ref, lnb_ref)
    nconv_ref[0] = conv_ext[CONV_HEAD + ts - CONV_BUF:CONV_HEAD + ts, :]
    conv_ext[0:CONV_HEAD, :] = conv_ext[ts:ts + CONV_HEAD, :]

    pa = _dot(h, win_ref[:, 0:2 * W])
    p_in = pa[:, :W]
    p_gate = pa[:, W:]
    pool_ext[POOL_HEAD:POOL_HEAD + ts, :] = p_in
    pos1 = (lax.broadcasted_iota(jnp.int32, (ts, POOL_GROUP_W), 0) + s * ts + 1).astype(F32)
    accs, cnts = [], []
    for g, win in enumerate(POOL_WINDOWS):
        c0 = g * POOL_GROUP_W
        acc = pool_ext[POOL_HEAD:POOL_HEAD + ts, c0:c0 + POOL_GROUP_W]
        for j in range(1, win):
            acc = acc + pool_ext[POOL_HEAD - j:POOL_HEAD - j + ts, c0:c0 + POOL_GROUP_W]
        accs.append(acc)
        cnts.append(jnp.minimum(pos1, float(win)))
    a = _pool_finish(accs, p_in, p_gate, cnts, poolw_ref, pscale_ref)
    npool_ref[0] = pool_ext[POOL_HEAD + ts - POOL_BUF:POOL_HEAD + ts, :]
    pool_ext[0:POOL_HEAD, :] = pool_ext[ts:ts + POOL_HEAD, :]

    pc = _dot(h, win_ref[:, 5 * W:7 * W])
    q = pc[:, :W]
    x_gate = pc[:, W:]
    scale = 1.0 / math.sqrt(XHEAD_DIM)
    outs = []
    for hd in range(N_XHEADS):
        c0 = hd * XHEAD_DIM
        sc = _dot(q[:, c0:c0 + XHEAD_DIM].astype(BF16), kt_s[c0:c0 + XHEAD_DIM, :]) * scale
        e = jnp.exp(sc - jnp.max(sc, axis=-1, keepdims=True))
        l = jnp.sum(e, axis=-1, keepdims=True)
        outs.append(_dot(e.astype(BF16), v_s[:, c0:c0 + XHEAD_DIM]) / l)
    cattn = jnp.concatenate(outs, axis=-1) * _silu(x_gate)

    merged = None
    for n, br in enumerate((a, bconv, cattn)):
        term = gate_s[:, n * D_MODEL:(n + 1) * D_MODEL] * _dot(br.astype(BF16), wbr_ref[n])
        merged = term if merged is None else merged + term
    y = _dot(merged.astype(BF16), wout_ref[...])
    y_ref[0] = x + _rmsnorm(y, gpost_ref[...])


def _sample_kernel(x_ref, spool_ref, sconv_ref, k_ref, v_ref, gpre_ref, gpost_ref, win_ref,
                   poolw_ref, pscale_ref, convw_ref, convb_ref, lng_ref, lnb_ref, wbr_ref, wout_ref,
                   y_ref, npool_ref, nconv_ref, pool_ext, conv_ext, q_s, o_s, v_buf, v_sem, *, nseq, layer):
    W = BRANCH_W
    T = nseq * DEC_SEQ
    n_slabs = W // LANES
    v_copy = pltpu.make_async_copy(
        v_ref.at[layer, pl.ds(pl.program_id(0) * nseq, nseq)], v_buf, v_sem)
    v_copy.start()
    x = x_ref[...].reshape(T, D_MODEL)
    h = _rmsnorm(x, gpre_ref[...]).astype(BF16)

    pa = _dot(h, win_ref[:, 0:2 * W])
    p_in = pa[:, :W]
    p_gate = pa[:, W:]
    pos1 = (lax.broadcasted_iota(jnp.int32, (DEC_SEQ, nseq, POOL_GROUP_W), 0) + PAST_LEN + 1).astype(F32)
    pos1 = pos1.reshape(T, POOL_GROUP_W)
    accs, cnts = [], []
    for g, win in enumerate(POOL_WINDOWS):
        c0 = g * POOL_GROUP_W
        pool_ext[g, 0:POOL_BUF * nseq, :] = spool_ref[:, :, c0:c0 + POOL_GROUP_W].reshape(
            POOL_BUF * nseq, POOL_GROUP_W)
        pool_ext[g, POOL_BUF * nseq:(POOL_BUF + DEC_SEQ) * nseq, :] = p_in[:, c0:c0 + POOL_GROUP_W]
        acc = None
        for i in range(win):
            r0 = (POOL_BUF - i) * nseq
            rows = pool_ext[g, r0:r0 + T, :]
            acc = rows if acc is None else acc + rows
        accs.append(acc)
        cnts.append(jnp.minimum(pos1, float(win)))
    a = _pool_finish(accs, p_in, p_gate, cnts, poolw_ref, pscale_ref)
    for sq in range(nseq):
        for g in range(n_slabs):
            npool_ref[sq, :, g * LANES:(g + 1) * LANES] = pool_ext[
                g, pl.ds(DEC_SEQ * nseq + sq, POOL_BUF, stride=nseq), :]

    pb = _dot(h, win_ref[:, 2 * W:5 * W])
    u = pb[:, :W] * _sigmoid(pb[:, W:2 * W])
    c_gate = pb[:, 2 * W:]
    slabs = []
    for c in range(n_slabs):
        c0 = c * LANES
        conv_ext[c, 0:CONV_BUF * nseq, :] = sconv_ref[:, :, c0:c0 + LANES].reshape(CONV_BUF * nseq, LANES)
        conv_ext[c, CONV_BUF * nseq:(CONV_BUF + DEC_SEQ) * nseq, :] = u[:, c0:c0 + LANES]
        acc = None
        for k in range(CONV_WIDTH):
            term = conv_ext[c, k * nseq:k * nseq + T, :] * convw_ref[k:k + 1, c0:c0 + LANES]
            acc = term if acc is None else acc + term
        slabs.append(acc)
    bconv = _conv_finish(jnp.concatenate(slabs, axis=-1), c_gate, convb_ref, lng_ref, lnb_ref)
    for sq in range(nseq):
        for c in range(n_slabs):
            nconv_ref[sq, :, c * LANES:(c + 1) * LANES] = conv_ext[
                c, pl.ds(DEC_SEQ * nseq + sq, CONV_BUF, stride=nseq), :]

    pc = _dot(h, win_ref[:, 5 * W:7 * W])
    x_gate = pc[:, W:]
    for hd in range(N_XHEADS):
        q_s[hd] = pc[:, hd * XHEAD_DIM:(hd + 1) * XHEAD_DIM]
    scale = 1.0 / math.sqrt(XHEAD_DIM)
    nq = N_XHEADS * DEC_SEQ
    nk = N_MEM * N_XHEADS
    q4 = jnp.stack([
        jnp.concatenate([q_s[hd, pl.ds(sq, DEC_SEQ, stride=nseq), :] for hd in range(N_XHEADS)], axis=0)
        for sq in range(nseq)]).astype(BF16)
    sc = jnp.einsum('gqd,gkd->gqk', q4, k_ref[...].astype(BF16), preferred_element_type=F32) * scale
    q_head = lax.broadcasted_iota(jnp.int32, (nq, nk), 0) // DEC_SEQ
    k_head = lax.broadcasted_iota(jnp.int32, (nq, nk), 1) % N_XHEADS
    sc = jnp.where((q_head == k_head)[None], sc, MASKED_SCORE)
    e = jnp.exp(sc - jnp.max(sc, axis=-1, keepdims=True))
    l = jnp.sum(e, axis=-1, keepdims=True)
    v_copy.wait()
    o = jnp.einsum('gqk,gkd->gqd', e.astype(BF16), v_buf[...].astype(BF16),
                   preferred_element_type=F32) / l
    for sq in range(nseq):
        for hd in range(N_XHEADS):
            o_s[hd, pl.ds(sq, DEC_SEQ, stride=nseq), :] = o[sq, hd * DEC_SEQ:(hd + 1) * DEC_SEQ, :]
    cattn = jnp.concatenate([o_s[hd] for hd in range(N_XHEADS)], axis=-1) * _silu(x_gate)

    y = _merge_out(x, h, (a, bconv, cattn), win_ref, wbr_ref, wout_ref, gpost_ref)
    y_ref[...] = y.reshape(DEC_SEQ, nseq, D_MODEL)


def _layer_spec(shape, layer):
    zeros = (0,) * len(shape)
    return pl.BlockSpec((None,) + shape, lambda *_: (layer,) + zeros, pipeline_mode=pl.Buffered(1))


def _prompt_layer(x, mem, layer, gpre, gpost, gmem, wkv, win, poolw, pscale, convw, convb, lng, lnb,
                  wbr, wout):
    B, S, D = x.shape
    W = BRANCH_W
    ts = PROMPT_TILE
    grid = (B, S // ts)
    small = lambda width: _layer_spec((1, width), layer)
    in_specs = [
        pl.BlockSpec((1, ts, D), lambda b, s: (b, s, 0)),
        pl.BlockSpec((1, N_MEM, D), lambda b, s: (b, 0, 0)),
        small(D), small(D), small(D),
        _layer_spec((D, 2 * W), layer),
        _layer_spec(win.shape[1:], layer),
        _layer_spec(poolw.shape[1:], layer),
        small(W),
        _layer_spec((CONV_WIDTH, W), layer),
        small(W), small(W), small(W),
        _layer_spec(wbr.shape[1:], layer),
        _layer_spec((D, D), layer),
    ]
    out_specs = [
        pl.BlockSpec((1, ts, D), lambda b, s: (b, s, 0)),
        pl.BlockSpec((1, POOL_BUF, W), lambda b, s: (b, 0, 0)),
        pl.BlockSpec((1, CONV_BUF, W), lambda b, s: (b, 0, 0)),
        pl.BlockSpec((1, N_MEM, W), lambda b, s: (b, 0, 0)),
        pl.BlockSpec((1, N_MEM, W), lambda b, s: (b, 0, 0)),
    ]
    out_shape = [
        jax.ShapeDtypeStruct((B, S, D), F32),
        jax.ShapeDtypeStruct((B, POOL_BUF, W), F32),
        jax.ShapeDtypeStruct((B, CONV_BUF, W), F32),
        jax.ShapeDtypeStruct((B, N_MEM, W), F32),
        jax.ShapeDtypeStruct((B, N_MEM, W), F32),
    ]
    scratch = [
        pltpu.VMEM((POOL_HEAD + ts, W), F32),
        pltpu.VMEM((CONV_HEAD + ts, W), F32),
        pltpu.VMEM((SUBLANES - 1, ts + CONV_HEAD - SUBLANES, W), F32),
        pltpu.VMEM((ts, N_BRANCH * D_MODEL), F32),
        pltpu.VMEM((W, N_MEM), BF16),
        pltpu.VMEM((N_MEM, W), BF16),
    ]
    return pl.pallas_call(
        functools.partial(_prompt_kernel, ts=ts),
        grid=grid, in_specs=in_specs, out_specs=out_specs, out_shape=out_shape,
        scratch_shapes=scratch, name=f"prompt_layer{layer}",
        compiler_params=pltpu.CompilerParams(
            dimension_semantics=("arbitrary", "arbitrary"), vmem_limit_bytes=VMEM_LIMIT),
    )(x, mem, gpre, gpost, gmem, wkv, win, poolw, pscale, convw, convb, lng, lnb, wbr, wout)


def _sample_layer(x, spool, sconv, ck, cv, layer, gpre, gpost, win, poolw, pscale, convw, convb, lng,
                  lnb, wbr, wout):
    nt, nb, D = x.shape
    W = BRANCH_W
    nseq = SAMPLE_SEQS
    T = nseq * nt
    n_slabs = W // LANES
    small = lambda width: _layer_spec((1, width), layer)
    state_in = lambda rows: pl.BlockSpec((None, rows, nseq, W), lambda g: (layer, 0, g, 0))
    cache = pl.BlockSpec((None, nseq, N_MEM * N_XHEADS, XHEAD_DIM), lambda g: (layer, g, 0, 0))
    in_specs = [
        pl.BlockSpec((nt, nseq, D), lambda g: (0, g, 0)),
        state_in(POOL_BUF), state_in(CONV_BUF), cache, pl.BlockSpec(memory_space=pl.ANY),
        small(D), small(D),
        _layer_spec(win.shape[1:], layer),
        _layer_spec(poolw.shape[1:], layer),
        small(W),
        _layer_spec((CONV_WIDTH, W), layer),
        small(W), small(W), small(W),
        _layer_spec(wbr.shape[1:], layer),
        _layer_spec((D, D), layer),
    ]
    out_specs = [
        pl.BlockSpec((nt, nseq, D), lambda g: (0, g, 0)),
        pl.BlockSpec((nseq, POOL_BUF, W), lambda g: (g, 0, 0)),
        pl.BlockSpec((nseq, CONV_BUF, W), lambda g: (g, 0, 0)),
    ]
    out_shape = [
        jax.ShapeDtypeStruct((nt, nb, D), F32),
        jax.ShapeDtypeStruct((nb, POOL_BUF, W), F32),
        jax.ShapeDtypeStruct((nb, CONV_BUF, W), F32),
    ]
    scratch = [
        pltpu.VMEM((n_slabs, (POOL_BUF + nt) * nseq, LANES), F32),
        pltpu.VMEM((n_slabs, (CONV_BUF + nt) * nseq, LANES), F32),
        pltpu.VMEM((N_XHEADS, T, XHEAD_DIM), F32),
        pltpu.VMEM((N_XHEADS, T, XHEAD_DIM), F32),
        pltpu.VMEM((nseq, N_MEM * N_XHEADS, XHEAD_DIM), F32),
        pltpu.SemaphoreType.DMA(()),
    ]
    return pl.pallas_call(
        functools.partial(_sample_kernel, nseq=nseq, layer=layer),
        grid=(nb // nseq,), in_specs=in_specs, out_specs=out_specs, out_shape=out_shape,
        scratch_shapes=scratch, name=f"sample_layer{layer}",
        compiler_params=pltpu.CompilerParams(
            dimension_semantics=("arbitrary",), vmem_limit_bytes=VMEM_LIMIT),
    )(x, spool, sconv, ck, cv, gpre, gpost, win, poolw, pscale, convw, convb, lng, lnb, wbr, wout)


def kernel(x_prompt, x_sample, state_pool, state_conv, cache_mem_k, cache_mem_v, mem_prompt,
           norm_pre, norm_post, mem_norm, w_mem_kv, w_in, pool_w, pool_scale, conv_w, conv_b,
           conv_ln_g, conv_ln_b, w_branch, w_out):
    W = BRANCH_W
    nb, nt, D = x_sample.shape
    row = lambda p: p.reshape(DEPTH, 1, p.shape[-1])
    gpre, gpost, gmem = row(norm_pre), row(norm_post), row(mem_norm)
    pscale, convb, lng, lnb = row(pool_scale), row(conv_b), row(conv_ln_g), row(conv_ln_b)
    wkv, win, poolw = w_mem_kv.astype(BF16), w_in.astype(BF16), pool_w.astype(BF16)
    wbr, wout = w_branch.astype(BF16), w_out.astype(BF16)
    ck = cache_mem_k.reshape(DEPTH, nb, N_MEM * N_XHEADS, XHEAD_DIM)
    cv = cache_mem_v.reshape(DEPTH, nb, N_MEM * N_XHEADS, XHEAD_DIM)

    spool_t = state_pool.transpose(0, 2, 1, 3)
    sconv_t = state_conv.transpose(0, 2, 1, 3)
    xs = x_sample.transpose(1, 0, 2)

    xp = x_prompt
    pool_p, conv_p, mk_p, mv_p, pool_s, conv_s = [], [], [], [], [], []
    for i in range(DEPTH):
        xp, npool, nconv, mk, mv = _prompt_layer(
            xp, mem_prompt, i, gpre, gpost, gmem, wkv, win, poolw, pscale, conv_w, convb, lng, lnb,
            wbr, wout)
        pool_p.append(npool)
        conv_p.append(nconv)
        mk_p.append(mk.reshape(mk.shape[0], N_MEM, N_XHEADS, XHEAD_DIM))
        mv_p.append(mv.reshape(mv.shape[0], N_MEM, N_XHEADS, XHEAD_DIM))
        xs, nspool, nsconv = _sample_layer(
            xs, spool_t, sconv_t, ck, cv, i, gpre, gpost, win, poolw, pscale, conv_w, convb,
            lng, lnb, wbr, wout)
        pool_s.append(nspool)
        conv_s.append(nsconv)
    return (xp, xs.transpose(1, 0, 2), jnp.stack(pool_p), jnp.stack(conv_p), jnp.stack(mk_p),
            jnp.stack(mv_p), jnp.stack(pool_s), jnp.stack(conv_s))
```

```python
import functools
import math

import jax
import jax.numpy as jnp
from jax import lax
from jax.experimental import pallas as pl
from jax.experimental.pallas import tpu as pltpu

D_MODEL = 1024
DEPTH = 2
DEC_SEQ = 8
PAST_LEN = 16384
BRANCH_W = D_MODEL // 2
N_BRANCH = 3
POOL_WINDOWS = (2, 4, 8, 16)
POOL_GROUP_W = BRANCH_W // len(POOL_WINDOWS)
POOL_BUF = max(POOL_WINDOWS) - 1
CONV_WIDTH = 31
CONV_BUF = CONV_WIDTH - 1
N_MEM = 256
N_XHEADS = 4
XHEAD_DIM = BRANCH_W // N_XHEADS
N_IN_SLICES = 7
EPS = 1e-6

LANES = 128
SUBLANES = 8
POOL_PAD = 8
POOL_HEAD = POOL_PAD + 16
CONV_HEAD = 32
PROMPT_TILE = 256
SAMPLE_SEQS = 16
VMEM_LIMIT = 60 * 1024 * 1024
NEG_LOG2E = -1.0 / math.log(2.0)
MASKED_SCORE = -1e30

F32 = jnp.float32
BF16 = jnp.bfloat16


def _sigmoid(x):
    return 1.0 / (1.0 + jnp.exp2(x * NEG_LOG2E))


def _silu(x):
    return x * _sigmoid(x)


def _rmsnorm(x, g):
    return x * lax.rsqrt(jnp.mean(x * x, axis=-1, keepdims=True) + EPS) * g


def _dot(a, b):
    return jnp.dot(a, b, preferred_element_type=F32)


def _pool_finish(acc_parts, p_in, p_gate, cnt_parts, poolw_ref, pscale_ref):
    ys = []
    for g in range(len(POOL_WINDOWS)):
        c0 = g * POOL_GROUP_W
        mixed = acc_parts[g] / cnt_parts[g] - p_in[:, c0:c0 + POOL_GROUP_W]
        ys.append(_dot(mixed.astype(BF16), poolw_ref[g]))
    y = jnp.concatenate(ys, axis=-1)
    return y * pscale_ref[...] * _silu(p_gate)


def _conv_finish(cv, c_gate, convb_ref, lng_ref, lnb_ref):
    cv = cv + convb_ref[...]
    mu = jnp.mean(cv, axis=-1, keepdims=True)
    d = cv - mu
    var = jnp.mean(d * d, axis=-1, keepdims=True)
    r = d * lax.rsqrt(var + EPS) * lng_ref[...] + lnb_ref[...]
    return _silu(r) * _silu(c_gate)


def _merge_out(x, h, branches, win_ref, wbr_ref, wout_ref, gpost_ref):
    merged = None
    base = N_IN_SLICES * BRANCH_W
    for n, br in enumerate(branches):
        proj = _dot(br.astype(BF16), wbr_ref[n])
        gate = _sigmoid(_dot(h, win_ref[:, base + n * D_MODEL:base + (n + 1) * D_MODEL]))
        term = gate * proj
        merged = term if merged is None else merged + term
    y = _dot(merged.astype(BF16), wout_ref[...])
    return x + _rmsnorm(y, gpost_ref[...])


def _prompt_kernel(x_ref, mem_ref, gpre_ref, gpost_ref, gmem_ref, wkv_ref, win_ref, poolw_ref,
                   pscale_ref, convw_ref, convb_ref, lng_ref, lnb_ref, wbr_ref, wout_ref,
                   y_ref, npool_ref, nconv_ref, mk_ref, mv_ref,
                   pool_ext, pool_tmp, conv_ext, conv_sh, gate_s, kt_s, v_s, *, ts):
    W = BRANCH_W
    s = pl.program_id(1)

    @pl.when(s == 0)
    def _():
        pool_ext[0:POOL_HEAD, :] = jnp.zeros((POOL_HEAD, W), F32)
        pool_tmp[:, 0:POOL_PAD, :] = jnp.zeros((pool_tmp.shape[0], POOL_PAD, POOL_GROUP_W), F32)
        conv_ext[0:CONV_HEAD, :] = jnp.zeros((CONV_HEAD, W), F32)
        hm = _rmsnorm(mem_ref[0], gmem_ref[...]).astype(BF16)
        kv = _dot(hm, wkv_ref[...])
        k = kv[:, :W]
        v = kv[:, W:]
        for hd in range(N_XHEADS):
            rows = pl.ds(hd, N_MEM, stride=N_XHEADS)
            mk_ref[0, rows, :] = k[:, hd * XHEAD_DIM:(hd + 1) * XHEAD_DIM]
            mv_ref[0, rows, :] = v[:, hd * XHEAD_DIM:(hd + 1) * XHEAD_DIM]
        kt_s[...] = k.T.astype(BF16)
        v_s[...] = v.astype(BF16)

    x = x_ref[0]
    h = _rmsnorm(x, gpre_ref[...]).astype(BF16)

    pb = _dot(h, win_ref[:, 2 * W:5 * W])
    u = pb[:, :W] * _sigmoid(pb[:, W:2 * W])
    c_gate = pb[:, 2 * W:]
    conv_ext[CONV_HEAD:CONV_HEAD + ts, :] = u
    for r in range(1, SUBLANES):
        conv_sh[r - 1, :, :] = conv_ext[r:r + ts + CONV_HEAD - SUBLANES, :]

    off = CONV_HEAD - CONV_BUF
    gate_base = N_IN_SLICES * W
    n_slabs = W // LANES
    gate_chunk = N_BRANCH * D_MODEL // n_slabs
    slabs = []
    for c in range(n_slabs):
        g0 = c * gate_chunk
        gate_s[:, g0:g0 + gate_chunk] = _sigmoid(
            _dot(h, win_ref[:, gate_base + g0:gate_base + g0 + gate_chunk]))
        c0 = c * LANES
        acc = None
        for k in range(CONV_WIDTH):
            a8, r = divmod(off + k, SUBLANES)
            if r == 0:
                rows = conv_ext[a8 * SUBLANES:a8 * SUBLANES + ts, c0:c0 + LANES]
            else:
                rows = conv_sh[r - 1, a8 * SUBLANES:a8 * SUBLANES + ts, c0:c0 + LANES]
            term = rows * convw_ref[k:k + 1, c0:c0 + LANES]
            acc = term if acc is None else acc + term
        slabs.append(acc)
    bconv = _conv_finish(jnp.concatenate(slabs, axis=-1), c_gate, convb_ref, lng_ref, lnb_ref)
    nconv_ref[0] = conv_ext[CONV_HEAD + ts - CONV_BUF:CONV_HEAD + ts, :]
    conv_ext[0:CONV_HEAD, :] = conv_ext[ts:ts + CONV_HEAD, :]

    pa = _dot(h, win_ref[:, 0:2 * W])
    p_in = pa[:, :W]
    p_gate = pa[:, W:]
    pool_ext[POOL_HEAD:POOL_HEAD + ts, :] = p_in
    pos1 = (lax.broadcasted_iota(jnp.int32, (ts, POOL_GROUP_W), 0) + s * ts + 1).astype(F32)
    accs, cnts = [], []
    for g, win in enumerate(POOL_WINDOWS):
        c0 = g * POOL_GROUP_W
        level, shift = None, 1
        while shift < win:
            last = 2 * shift == win
            lo = POOL_HEAD if last else SUBLANES
            n = POOL_HEAD + ts - lo
            if level is None:
                cur = (pool_ext[lo:lo + n, c0:c0 + POOL_GROUP_W]
                       + pool_ext[lo - shift:lo - shift + n, c0:c0 + POOL_GROUP_W])
            else:
                cur = pool_tmp[level, lo:lo + n, :] + pool_tmp[level, lo - shift:lo - shift + n, :]
            if last:
                accs.append(cur)
            else:
                level = 0 if level is None else level + 1
                pool_tmp[level, lo:lo + n, :] = cur
            shift *= 2
        cnts.append(jnp.minimum(pos1, float(win)))
    a = _pool_finish(accs, p_in, p_gate, cnts, poolw_ref, pscale_ref)
    npool_ref[0] = pool_ext[POOL_HEAD + ts - POOL_BUF:POOL_HEAD + ts, :]
    pool_ext[POOL_PAD:POOL_HEAD, :] = pool_ext[ts + POOL_PAD:ts + POOL_HEAD, :]

    pc = _dot(h, win_ref[:, 5 * W:7 * W])
    q = pc[:, :W]
    x_gate = pc[:, W:]
    scale = 1.0 / math.sqrt(XHEAD_DIM)
    outs = []
    for hd in range(N_XHEADS):
        c0 = hd * XHEAD_DIM
        sc = _dot((q[:, c0:c0 + XHEAD_DIM] * scale).astype(BF16), kt_s[c0:c0 + XHEAD_DIM, :])
        e = jnp.exp(sc - jnp.max(sc, axis=-1, keepdims=True))
        l = jnp.sum(e, axis=-1, keepdims=True)
        outs.append(_dot(e.astype(BF16), v_s[:, c0:c0 + XHEAD_DIM]) / l)
    cattn = jnp.concatenate(outs, axis=-1) * _silu(x_gate)

    merged = None
    for n, br in enumerate((a, bconv, cattn)):
        term = gate_s[:, n * D_MODEL:(n + 1) * D_MODEL] * _dot(br.astype(BF16), wbr_ref[n])
        merged = term if merged is None else merged + term
    y = _dot(merged.astype(BF16), wout_ref[...])
    y_ref[0] = x + _rmsnorm(y, gpost_ref[...])


def _sample_kernel(x_ref, spool_ref, sconv_ref, k_ref, v_ref, gpre_ref, gpost_ref, win_ref,
                   poolw_ref, pscale_ref, convw_ref, convb_ref, lng_ref, lnb_ref, wbr_ref, wout_ref,
                   y_ref, npool_ref, nconv_ref, pool_ext, conv_ext, q_s, o_s, v_buf, v_sem, *, nseq, layer):
    W = BRANCH_W
    T = nseq * DEC_SEQ
    n_slabs = W // LANES
    v_copy = pltpu.make_async_copy(
        v_ref.at[layer, pl.ds(pl.program_id(0) * nseq, nseq)], v_buf, v_sem)
    v_copy.start()
    x = x_ref[...].reshape(T, D_MODEL)
    h = _rmsnorm(x, gpre_ref[...]).astype(BF16)

    pa = _dot(h, win_ref[:, 0:2 * W])
    p_in = pa[:, :W]
    p_gate = pa[:, W:]
    pos1 = (lax.broadcasted_iota(jnp.int32, (DEC_SEQ, nseq, POOL_GROUP_W), 0) + PAST_LEN + 1).astype(F32)
    pos1 = pos1.reshape(T, POOL_GROUP_W)
    accs, cnts = [], []
    for g, win in enumerate(POOL_WINDOWS):
        c0 = g * POOL_GROUP_W
        pool_ext[g, 0:POOL_BUF * nseq, :] = spool_ref[:, :, c0:c0 + POOL_GROUP_W].reshape(
            POOL_BUF * nseq, POOL_GROUP_W)
        pool_ext[g, POOL_BUF * nseq:(POOL_BUF + DEC_SEQ) * nseq, :] = p_in[:, c0:c0 + POOL_GROUP_W]
        acc = None
        for i in range(win):
            r0 = (POOL_BUF - i) * nseq
            rows = pool_ext[g, r0:r0 + T, :]
            acc = rows if acc is None else acc + rows
        accs.append(acc)
        cnts.append(jnp.minimum(pos1, float(win)))
    a = _pool_finish(accs, p_in, p_gate, cnts, poolw_ref, pscale_ref)
    for sq in range(nseq):
        for g in range(n_slabs):
            npool_ref[sq, :, g * LANES:(g + 1) * LANES] = pool_ext[
                g, pl.ds(DEC_SEQ * nseq + sq, POOL_BUF, stride=nseq), :]

    pb = _dot(h, win_ref[:, 2 * W:5 * W])
    u = pb[:, :W] * _sigmoid(pb[:, W:2 * W])
    c_gate = pb[:, 2 * W:]
    slabs = []
    for c in range(n_slabs):
        c0 = c * LANES
        conv_ext[c, 0:CONV_BUF * nseq, :] = sconv_ref[:, :, c0:c0 + LANES].reshape(CONV_BUF * nseq, LANES)
        conv_ext[c, CONV_BUF * nseq:(CONV_BUF + DEC_SEQ) * nseq, :] = u[:, c0:c0 + LANES]
        acc = None
        for k in range(CONV_WIDTH):
            term = conv_ext[c, k * nseq:k * nseq + T, :] * convw_ref[k:k + 1, c0:c0 + LANES]
            acc = term if acc is None else acc + term
        slabs.append(acc)
    bconv = _conv_finish(jnp.concatenate(slabs, axis=-1), c_gate, convb_ref, lng_ref, lnb_ref)
    for sq in range(nseq):
        for c in range(n_slabs):
            nconv_ref[sq, :, c * LANES:(c + 1) * LANES] = conv_ext[
                c, pl.ds(DEC_SEQ * nseq + sq, CONV_BUF, stride=nseq), :]

    pc = _dot(h, win_ref[:, 5 * W:7 * W])
    x_gate = pc[:, W:]
    for hd in range(N_XHEADS):
        q_s[hd] = pc[:, hd * XHEAD_DIM:(hd + 1) * XHEAD_DIM]
    scale = 1.0 / math.sqrt(XHEAD_DIM)
    nq = N_XHEADS * DEC_SEQ
    nk = N_MEM * N_XHEADS
    q4 = jnp.stack([
        jnp.concatenate([q_s[hd, pl.ds(sq, DEC_SEQ, stride=nseq), :] for hd in range(N_XHEADS)], axis=0)
        for sq in range(nseq)]).astype(BF16)
    sc = jnp.einsum('gqd,gkd->gqk', q4, k_ref[...].astype(BF16), preferred_element_type=F32) * scale
    q_head = lax.broadcasted_iota(jnp.int32, (nq, nk), 0) // DEC_SEQ
    k_head = lax.broadcasted_iota(jnp.int32, (nq, nk), 1) % N_XHEADS
    sc = jnp.where((q_head == k_head)[None], sc, MASKED_SCORE)
    e = jnp.exp(sc - jnp.max(sc, axis=-1, keepdims=True))
    l = jnp.sum(e, axis=-1, keepdims=True)
    v_copy.wait()
    o = jnp.einsum('gqk,gkd->gqd', e.astype(BF16), v_buf[...].astype(BF16),
                   preferred_element_type=F32) / l
    for sq in range(nseq):
        for hd in range(N_XHEADS):
            o_s[hd, pl.ds(sq, DEC_SEQ, stride=nseq), :] = o[sq, hd * DEC_SEQ:(hd + 1) * DEC_SEQ, :]
    cattn = jnp.concatenate([o_s[hd] for hd in range(N_XHEADS)], axis=-1) * _silu(x_gate)

    y = _merge_out(x, h, (a, bconv, cattn), win_ref, wbr_ref, wout_ref, gpost_ref)
    y_ref[...] = y.reshape(DEC_SEQ, nseq, D_MODEL)


def _layer_spec(shape, layer):
    zeros = (0,) * len(shape)
    return pl.BlockSpec((None,) + shape, lambda *_: (layer,) + zeros, pipeline_mode=pl.Buffered(1))


def _prompt_layer(x, mem, layer, gpre, gpost, gmem, wkv, win, poolw, pscale, convw, convb, lng, lnb,
                  wbr, wout):
    B, S, D = x.shape
    W = BRANCH_W
    ts = PROMPT_TILE
    grid = (B, S // ts)
    small = lambda width: _layer_spec((1, width), layer)
    in_specs = [
        pl.BlockSpec((1, ts, D), lambda b, s: (b, s, 0)),
        pl.BlockSpec((1, N_MEM, D), lambda b, s: (b, 0, 0)),
        small(D), small(D), small(D),
        _layer_spec((D, 2 * W), layer),
        _layer_spec(win.shape[1:], layer),
        _layer_spec(poolw.shape[1:], layer),
        small(W),
        _layer_spec((CONV_WIDTH, W), layer),
        small(W), small(W), small(W),
        _layer_spec(wbr.shape[1:], layer),
        _layer_spec((D, D), layer),
    ]
    out_specs = [
        pl.BlockSpec((1, ts, D), lambda b, s: (b, s, 0)),
        pl.BlockSpec((1, POOL_BUF, W), lambda b, s: (b, 0, 0)),
        pl.BlockSpec((1, CONV_BUF, W), lambda b, s: (b, 0, 0)),
        pl.BlockSpec((1, N_MEM * N_XHEADS, XHEAD_DIM), lambda b, s: (b, 0, 0)),
        pl.BlockSpec((1, N_MEM * N_XHEADS, XHEAD_DIM), lambda b, s: (b, 0, 0)),
    ]
    out_shape = [
        jax.ShapeDtypeStruct((B, S, D), F32),
        jax.ShapeDtypeStruct((B, POOL_BUF, W), F32),
        jax.ShapeDtypeStruct((B, CONV_BUF, W), F32),
        jax.ShapeDtypeStruct((B, N_MEM * N_XHEADS, XHEAD_DIM), F32),
        jax.ShapeDtypeStruct((B, N_MEM * N_XHEADS, XHEAD_DIM), F32),
    ]
    scratch = [
        pltpu.VMEM((POOL_HEAD + ts, W), F32),
        pltpu.VMEM((len(POOL_WINDOWS) - 1, POOL_HEAD + ts, POOL_GROUP_W), F32),
        pltpu.VMEM((CONV_HEAD + ts, W), F32),
        pltpu.VMEM((SUBLANES - 1, ts + CONV_HEAD - SUBLANES, W), F32),
        pltpu.VMEM((ts, N_BRANCH * D_MODEL), F32),
        pltpu.VMEM((W, N_MEM), BF16),
        pltpu.VMEM((N_MEM, W), BF16),
    ]
    return pl.pallas_call(
        functools.partial(_prompt_kernel, ts=ts),
        grid=grid, in_specs=in_specs, out_specs=out_specs, out_shape=out_shape,
        scratch_shapes=scratch, name=f"prompt_layer{layer}",
        compiler_params=pltpu.CompilerParams(
            dimension_semantics=("arbitrary", "arbitrary"), vmem_limit_bytes=VMEM_LIMIT),
    )(x, mem, gpre, gpost, gmem, wkv, win, poolw, pscale, convw, convb, lng, lnb, wbr, wout)


def _sample_layer(x, spool, sconv, ck, cv, layer, gpre, gpost, win, poolw, pscale, convw, convb, lng,
                  lnb, wbr, wout):
    nt, nb, D = x.shape
    W = BRANCH_W
    nseq = SAMPLE_SEQS
    T = nseq * nt
    n_slabs = W // LANES
    small = lambda width: _layer_spec((1, width), layer)
    state_in = lambda rows: pl.BlockSpec((None, rows, nseq, W), lambda g: (layer, 0, g, 0))
    cache = pl.BlockSpec((None, nseq, N_MEM * N_XHEADS, XHEAD_DIM), lambda g: (layer, g, 0, 0))
    in_specs = [
        pl.BlockSpec((nt, nseq, D), lambda g: (0, g, 0)),
        state_in(POOL_BUF), state_in(CONV_BUF), cache, pl.BlockSpec(memory_space=pl.ANY),
        small(D), small(D),
        _layer_spec(win.shape[1:], layer),
        _layer_spec(poolw.shape[1:], layer),
        small(W),
        _layer_spec((CONV_WIDTH, W), layer),
        small(W), small(W), small(W),
        _layer_spec(wbr.shape[1:], layer),
        _layer_spec((D, D), layer),
    ]
    out_specs = [
        pl.BlockSpec((nt, nseq, D), lambda g: (0, g, 0)),
        pl.BlockSpec((nseq, POOL_BUF, W), lambda g: (g, 0, 0)),
        pl.BlockSpec((nseq, CONV_BUF, W), lambda g: (g, 0, 0)),
    ]
    out_shape = [
        jax.ShapeDtypeStruct((nt, nb, D), F32),
        jax.ShapeDtypeStruct((nb, POOL_BUF, W), F32),
        jax.ShapeDtypeStruct((nb, CONV_BUF, W), F32),
    ]
    scratch = [
        pltpu.VMEM((n_slabs, (POOL_BUF + nt) * nseq, LANES), F32),
        pltpu.VMEM((n_slabs, (CONV_BUF + nt) * nseq, LANES), F32),
        pltpu.VMEM((N_XHEADS, T, XHEAD_DIM), F32),
        pltpu.VMEM((N_XHEADS, T, XHEAD_DIM), F32),
        pltpu.VMEM((nseq, N_MEM * N_XHEADS, XHEAD_DIM), F32),
        pltpu.SemaphoreType.DMA(()),
    ]
    return pl.pallas_call(
        functools.partial(_sample_kernel, nseq=nseq, layer=layer),
        grid=(nb // nseq,), in_specs=in_specs, out_specs=out_specs, out_shape=out_shape,
        scratch_shapes=scratch, name=f"sample_layer{layer}",
        compiler_params=pltpu.CompilerParams(
            dimension_semantics=("arbitrary",), vmem_limit_bytes=VMEM_LIMIT),
    )(x, spool, sconv, ck, cv, gpre, gpost, win, poolw, pscale, convw, convb, lng, lnb, wbr, wout)


def kernel(x_prompt, x_sample, state_pool, state_conv, cache_mem_k, cache_mem_v, mem_prompt,
           norm_pre, norm_post, mem_norm, w_mem_kv, w_in, pool_w, pool_scale, conv_w, conv_b,
           conv_ln_g, conv_ln_b, w_branch, w_out):
    W = BRANCH_W
    nb, nt, D = x_sample.shape
    row = lambda p: p.reshape(DEPTH, 1, p.shape[-1])
    gpre, gpost, gmem = row(norm_pre), row(norm_post), row(mem_norm)
    pscale, convb, lng, lnb = row(pool_scale), row(conv_b), row(conv_ln_g), row(conv_ln_b)
    wkv, win, poolw = w_mem_kv.astype(BF16), w_in.astype(BF16), pool_w.astype(BF16)
    wbr, wout = w_branch.astype(BF16), w_out.astype(BF16)
    ck = cache_mem_k.reshape(DEPTH, nb, N_MEM * N_XHEADS, XHEAD_DIM)
    cv = cache_mem_v.reshape(DEPTH, nb, N_MEM * N_XHEADS, XHEAD_DIM)

    spool_t = state_pool.transpose(0, 2, 1, 3)
    sconv_t = state_conv.transpose(0, 2, 1, 3)
    xs = x_sample.transpose(1, 0, 2)

    xp = x_prompt
    pool_p, conv_p, mk_p, mv_p, pool_s, conv_s = [], [], [], [], [], []
    for i in range(DEPTH):
        xp, npool, nconv, mk, mv = _prompt_layer(
            xp, mem_prompt, i, gpre, gpost, gmem, wkv, win, poolw, pscale, conv_w, convb, lng, lnb,
            wbr, wout)
        pool_p.append(npool)
        conv_p.append(nconv)
        mk_p.append(mk.reshape(mk.shape[0], N_MEM, N_XHEADS, XHEAD_DIM))
        mv_p.append(mv.reshape(mv.shape[0], N_MEM, N_XHEADS, XHEAD_DIM))
        xs, nspool, nsconv = _sample_layer(
            xs, spool_t, sconv_t, ck, cv, i, gpre, gpost, win, poolw, pscale, conv_w, convb,
            lng, lnb, wbr, wout)
        pool_s.append(nspool)
        conv_s.append(nsconv)
    return (xp, xs.transpose(1, 0, 2), jnp.stack(pool_p), jnp.stack(conv_p), jnp.stack(mk_p),
            jnp.stack(mv_p), jnp.stack(pool_s), jnp.stack(conv_s))
```

```python
import functools
import math

import jax
import jax.numpy as jnp
from jax import lax
from jax.experimental import pallas as pl
from jax.experimental.pallas import tpu as pltpu

D_MODEL = 1024
DEPTH = 2
DEC_SEQ = 8
PAST_LEN = 16384
BRANCH_W = D_MODEL // 2
N_BRANCH = 3
POOL_WINDOWS = (2, 4, 8, 16)
POOL_GROUP_W = BRANCH_W // len(POOL_WINDOWS)
POOL_BUF = max(POOL_WINDOWS) - 1
CONV_WIDTH = 31
CONV_BUF = CONV_WIDTH - 1
N_MEM = 256
N_XHEADS = 4
XHEAD_DIM = BRANCH_W // N_XHEADS
N_IN_SLICES = 7
EPS = 1e-6

LANES = 128
SUBLANES = 8
POOL_PAD = 8
POOL_HEAD = POOL_PAD + 16
CONV_HEAD = 32
PROMPT_TILE = 256
SAMPLE_SEQS = 16
V_CHUNKS = 4
VMEM_LIMIT = 60 * 1024 * 1024
NEG_LOG2E = -1.0 / math.log(2.0)
MASKED_SCORE = -1e30

F32 = jnp.float32
BF16 = jnp.bfloat16


def _sigmoid(x):
    return 1.0 / (1.0 + jnp.exp2(x * NEG_LOG2E))


def _silu(x):
    return x * _sigmoid(x)


def _rmsnorm(x, g):
    return x * lax.rsqrt(jnp.mean(x * x, axis=-1, keepdims=True) + EPS) * g


def _dot(a, b):
    return jnp.dot(a, b, preferred_element_type=F32)


def _pool_finish(acc_parts, p_in, p_gate, cnt_parts, poolw_ref, pscale_ref):
    ys = []
    for g in range(len(POOL_WINDOWS)):
        c0 = g * POOL_GROUP_W
        mixed = acc_parts[g] / cnt_parts[g] - p_in[:, c0:c0 + POOL_GROUP_W]
        ys.append(_dot(mixed.astype(BF16), poolw_ref[g]))
    y = jnp.concatenate(ys, axis=-1)
    return y * pscale_ref[...] * _silu(p_gate)


def _conv_finish(cv, c_gate, convb_ref, lng_ref, lnb_ref):
    cv = cv + convb_ref[...]
    mu = jnp.mean(cv, axis=-1, keepdims=True)
    d = cv - mu
    var = jnp.mean(d * d, axis=-1, keepdims=True)
    r = d * lax.rsqrt(var + EPS) * lng_ref[...] + lnb_ref[...]
    return _silu(r) * _silu(c_gate)


def _merge_out(x, h, branches, win_ref, wbr_ref, wout_ref, gpost_ref):
    merged = None
    base = N_IN_SLICES * BRANCH_W
    for n, br in enumerate(branches):
        proj = _dot(br.astype(BF16), wbr_ref[n])
        gate = _sigmoid(_dot(h, win_ref[:, base + n * D_MODEL:base + (n + 1) * D_MODEL]))
        term = gate * proj
        merged = term if merged is None else merged + term
    y = _dot(merged.astype(BF16), wout_ref[...])
    return x + _rmsnorm(y, gpost_ref[...])


def _own_layer(ref, layer, first):
    return ref.at[layer] if first else ref


def _zero_other_layers(ref, layer):
    for other in range(ref.shape[0]):
        if other != layer:
            ref[other] = jnp.zeros(ref.shape[1:], ref.dtype)


def _prompt_kernel(x_ref, mem_ref, gpre_ref, gpost_ref, gmem_ref, wkv_ref, win_ref, poolw_ref,
                   pscale_ref, convw_ref, convb_ref, lng_ref, lnb_ref, wbr_ref, wout_ref,
                   y_ref, npool_ref, nconv_ref, mk_ref, mv_ref,
                   pool_ext, pool_tmp, conv_ext, conv_sh, gate_s, kt_s, v_s, *, ts, layer, first):
    W = BRANCH_W
    s = pl.program_id(1)
    per_layer = (npool_ref, nconv_ref, mk_ref, mv_ref)
    npool_ref, nconv_ref, mk_ref, mv_ref = (_own_layer(r, layer, first) for r in per_layer)

    @pl.when(s == 0)
    def _():
        if first:
            for r in per_layer:
                _zero_other_layers(r, layer)
        pool_ext[0:POOL_HEAD, :] = jnp.zeros((POOL_HEAD, W), F32)
        pool_tmp[:, 0:POOL_PAD, :] = jnp.zeros((pool_tmp.shape[0], POOL_PAD, POOL_GROUP_W), F32)
        conv_ext[0:CONV_HEAD, :] = jnp.zeros((CONV_HEAD, W), F32)
        hm = _rmsnorm(mem_ref[0], gmem_ref[...]).astype(BF16)
        kv = _dot(hm, wkv_ref[...])
        k = kv[:, :W]
        v = kv[:, W:]
        for hd in range(N_XHEADS):
            rows = pl.ds(hd, N_MEM, stride=N_XHEADS)
            mk_ref[0, rows, :] = k[:, hd * XHEAD_DIM:(hd + 1) * XHEAD_DIM]
            mv_ref[0, rows, :] = v[:, hd * XHEAD_DIM:(hd + 1) * XHEAD_DIM]
        kt_s[...] = k.T.astype(BF16)
        v_s[...] = v.astype(BF16)

    x = x_ref[0]
    h = _rmsnorm(x, gpre_ref[...]).astype(BF16)

    pb = _dot(h, win_ref[:, 2 * W:5 * W])
    u = pb[:, :W] * _sigmoid(pb[:, W:2 * W])
    c_gate = pb[:, 2 * W:]
    conv_ext[CONV_HEAD:CONV_HEAD + ts, :] = u
    for r in range(1, SUBLANES):
        conv_sh[r - 1, :, :] = conv_ext[r:r + ts + CONV_HEAD - SUBLANES, :]

    off = CONV_HEAD - CONV_BUF
    gate_base = N_IN_SLICES * W
    n_slabs = W // LANES
    gate_chunk = N_BRANCH * D_MODEL // n_slabs
    slabs = []
    for c in range(n_slabs):
        g0 = c * gate_chunk
        gate_s[:, g0:g0 + gate_chunk] = _sigmoid(
            _dot(h, win_ref[:, gate_base + g0:gate_base + g0 + gate_chunk]))
        c0 = c * LANES
        acc = None
        for k in range(CONV_WIDTH):
            a8, r = divmod(off + k, SUBLANES)
            if r == 0:
                rows = conv_ext[a8 * SUBLANES:a8 * SUBLANES + ts, c0:c0 + LANES]
            else:
                rows = conv_sh[r - 1, a8 * SUBLANES:a8 * SUBLANES + ts, c0:c0 + LANES]
            term = rows * convw_ref[k:k + 1, c0:c0 + LANES]
            acc = term if acc is None else acc + term
        slabs.append(acc)
    bconv = _conv_finish(jnp.concatenate(slabs, axis=-1), c_gate, convb_ref, lng_ref, lnb_ref)
    nconv_ref[0] = conv_ext[CONV_HEAD + ts - CONV_BUF:CONV_HEAD + ts, :]
    conv_ext[0:CONV_HEAD, :] = conv_ext[ts:ts + CONV_HEAD, :]

    pa = _dot(h, win_ref[:, 0:2 * W])
    p_in = pa[:, :W]
    p_gate = pa[:, W:]
    pool_ext[POOL_HEAD:POOL_HEAD + ts, :] = p_in
    pos1 = (lax.broadcasted_iota(jnp.int32, (ts, POOL_GROUP_W), 0) + s * ts + 1).astype(F32)
    accs, cnts = [], []
    for g, win in enumerate(POOL_WINDOWS):
        c0 = g * POOL_GROUP_W
        level, shift = None, 1
        while shift < win:
            last = 2 * shift == win
            lo = POOL_HEAD if last else SUBLANES
            n = POOL_HEAD + ts - lo
            if level is None:
                cur = (pool_ext[lo:lo + n, c0:c0 + POOL_GROUP_W]
                       + pool_ext[lo - shift:lo - shift + n, c0:c0 + POOL_GROUP_W])
            else:
                cur = pool_tmp[level, lo:lo + n, :] + pool_tmp[level, lo - shift:lo - shift + n, :]
            if last:
                accs.append(cur)
            else:
                level = 0 if level is None else level + 1
                pool_tmp[level, lo:lo + n, :] = cur
            shift *= 2
        cnts.append(jnp.minimum(pos1, float(win)))
    a = _pool_finish(accs, p_in, p_gate, cnts, poolw_ref, pscale_ref)
    npool_ref[0] = pool_ext[POOL_HEAD + ts - POOL_BUF:POOL_HEAD + ts, :]
    pool_ext[POOL_PAD:POOL_HEAD, :] = pool_ext[ts + POOL_PAD:ts + POOL_HEAD, :]

    pc = _dot(h, win_ref[:, 5 * W:7 * W])
    q = pc[:, :W]
    x_gate = pc[:, W:]
    scale = 1.0 / math.sqrt(XHEAD_DIM)
    outs = []
    for hd in range(N_XHEADS):
        c0 = hd * XHEAD_DIM
        sc = _dot((q[:, c0:c0 + XHEAD_DIM] * scale).astype(BF16), kt_s[c0:c0 + XHEAD_DIM, :])
        e = jnp.exp(sc - jnp.max(sc, axis=-1, keepdims=True))
        l = jnp.sum(e, axis=-1, keepdims=True)
        outs.append(_dot(e.astype(BF16), v_s[:, c0:c0 + XHEAD_DIM]) / l)
    cattn = jnp.concatenate(outs, axis=-1) * _silu(x_gate)

    merged = None
    for n, br in enumerate((a, bconv, cattn)):
        term = gate_s[:, n * D_MODEL:(n + 1) * D_MODEL] * _dot(br.astype(BF16), wbr_ref[n])
        merged = term if merged is None else merged + term
    y = _dot(merged.astype(BF16), wout_ref[...])
    y_ref[0] = x + _rmsnorm(y, gpost_ref[...])


def _sample_kernel(x_ref, spool_ref, sconv_ref, k_ref, v_ref, gpre_ref, gpost_ref, win_ref,
                   poolw_ref, pscale_ref, convw_ref, convb_ref, lng_ref, lnb_ref, wbr_ref, wout_ref,
                   y_ref, npool_ref, nconv_ref, pool_ext, conv_ext, q_s, o_s, v_buf, v_sem,
                   *, nseq, layer, first):
    W = BRANCH_W
    T = nseq * DEC_SEQ
    n_slabs = W // LANES
    if first:
        _zero_other_layers(npool_ref, layer)
        _zero_other_layers(nconv_ref, layer)
    npool_ref = _own_layer(npool_ref, layer, first)
    nconv_ref = _own_layer(nconv_ref, layer, first)
    per_chunk = nseq // V_CHUNKS

    def v_copy(c):
        first = pl.program_id(0) * nseq + c * per_chunk
        return pltpu.make_async_copy(v_ref.at[layer, pl.ds(first, per_chunk)],
                                     v_buf.at[pl.ds(c * per_chunk, per_chunk)], v_sem.at[c])

    for c in range(V_CHUNKS):
        v_copy(c).start(priority=c % 2)
    x = x_ref[...].reshape(T, D_MODEL)
    h = _rmsnorm(x, gpre_ref[...]).astype(BF16)

    pa = _dot(h, win_ref[:, 0:2 * W])
    p_in = pa[:, :W]
    p_gate = pa[:, W:]
    pos1 = (lax.broadcasted_iota(jnp.int32, (DEC_SEQ, nseq, POOL_GROUP_W), 0) + PAST_LEN + 1).astype(F32)
    pos1 = pos1.reshape(T, POOL_GROUP_W)
    accs, cnts = [], []
    for g, win in enumerate(POOL_WINDOWS):
        c0 = g * POOL_GROUP_W
        pool_ext[g, 0:POOL_BUF * nseq, :] = spool_ref[:, :, c0:c0 + POOL_GROUP_W].reshape(
            POOL_BUF * nseq, POOL_GROUP_W)
        pool_ext[g, POOL_BUF * nseq:(POOL_BUF + DEC_SEQ) * nseq, :] = p_in[:, c0:c0 + POOL_GROUP_W]
        acc = None
        for i in range(win):
            r0 = (POOL_BUF - i) * nseq
            rows = pool_ext[g, r0:r0 + T, :]
            acc = rows if acc is None else acc + rows
        accs.append(acc)
        cnts.append(jnp.minimum(pos1, float(win)))
    a = _pool_finish(accs, p_in, p_gate, cnts, poolw_ref, pscale_ref)
    for sq in range(nseq):
        for g in range(n_slabs):
            npool_ref[sq, :, g * LANES:(g + 1) * LANES] = pool_ext[
                g, pl.ds(DEC_SEQ * nseq + sq, POOL_BUF, stride=nseq), :]

    pb = _dot(h, win_ref[:, 2 * W:5 * W])
    u = pb[:, :W] * _sigmoid(pb[:, W:2 * W])
    c_gate = pb[:, 2 * W:]
    slabs = []
    for c in range(n_slabs):
        c0 = c * LANES
        conv_ext[c, 0:CONV_BUF * nseq, :] = sconv_ref[:, :, c0:c0 + LANES].reshape(CONV_BUF * nseq, LANES)
        conv_ext[c, CONV_BUF * nseq:(CONV_BUF + DEC_SEQ) * nseq, :] = u[:, c0:c0 + LANES]
        acc = None
        for k in range(CONV_WIDTH):
            term = conv_ext[c, k * nseq:k * nseq + T, :] * convw_ref[k:k + 1, c0:c0 + LANES]
            acc = term if acc is None else acc + term
        slabs.append(acc)
    bconv = _conv_finish(jnp.concatenate(slabs, axis=-1), c_gate, convb_ref, lng_ref, lnb_ref)
    for sq in range(nseq):
        for c in range(n_slabs):
            nconv_ref[sq, :, c * LANES:(c + 1) * LANES] = conv_ext[
                c, pl.ds(DEC_SEQ * nseq + sq, CONV_BUF, stride=nseq), :]

    pc = _dot(h, win_ref[:, 5 * W:7 * W])
    x_gate = pc[:, W:]
    for hd in range(N_XHEADS):
        q_s[hd] = pc[:, hd * XHEAD_DIM:(hd + 1) * XHEAD_DIM]
    scale = 1.0 / math.sqrt(XHEAD_DIM)
    nq = N_XHEADS * DEC_SEQ
    nk = N_MEM * N_XHEADS
    q4 = jnp.stack([
        jnp.concatenate([q_s[hd, pl.ds(sq, DEC_SEQ, stride=nseq), :] for hd in range(N_XHEADS)], axis=0)
        for sq in range(nseq)]).astype(BF16)
    sc = jnp.einsum('gqd,gkd->gqk', q4, k_ref[...].astype(BF16), preferred_element_type=F32) * scale
    q_head = lax.broadcasted_iota(jnp.int32, (nq, nk), 0) // DEC_SEQ
    k_head = lax.broadcasted_iota(jnp.int32, (nq, nk), 1) % N_XHEADS
    sc = jnp.where((q_head == k_head)[None], sc, MASKED_SCORE)
    e = jnp.exp(sc - jnp.max(sc, axis=-1, keepdims=True))
    l = jnp.sum(e, axis=-1, keepdims=True)
    for c in range(V_CHUNKS):
        v_copy(c).wait()
    o = jnp.einsum('gqk,gkd->gqd', e.astype(BF16), v_buf[...].astype(BF16),
                   preferred_element_type=F32) / l
    for sq in range(nseq):
        for hd in range(N_XHEADS):
            o_s[hd, pl.ds(sq, DEC_SEQ, stride=nseq), :] = o[sq, hd * DEC_SEQ:(hd + 1) * DEC_SEQ, :]
    cattn = jnp.concatenate([o_s[hd] for hd in range(N_XHEADS)], axis=-1) * _silu(x_gate)

    y = _merge_out(x, h, (a, bconv, cattn), win_ref, wbr_ref, wout_ref, gpost_ref)
    y_ref[...] = y.reshape(DEC_SEQ, nseq, D_MODEL)


def _layer_spec(shape, layer):
    zeros = (0,) * len(shape)
    return pl.BlockSpec((None,) + shape, lambda *_: (layer,) + zeros, pipeline_mode=pl.Buffered(1))


def _carry_layers(operands, in_specs, carried, first_output):
    aliases = {}
    for i, arr in enumerate(carried or ()):
        aliases[len(operands)] = first_output + i
        operands.append(arr)
        in_specs.append(pl.BlockSpec(memory_space=pl.ANY))
    return aliases


def _per_layer_spec(block, index_map, layer, first):
    if first:
        return pl.BlockSpec((DEPTH,) + block, lambda *i: (0,) + index_map(*i))
    return pl.BlockSpec((None,) + block, lambda *i: (layer,) + index_map(*i))


def _skip_carried(body, n_inputs, n_carried):
    def wrapped(*refs, **kw):
        return body(*refs[:n_inputs], *refs[n_inputs + n_carried:], **kw)
    return wrapped


def _prompt_layer(x, mem, layer, carried, gpre, gpost, gmem, wkv, win, poolw, pscale, convw, convb,
                  lng, lnb, wbr, wout):
    B, S, D = x.shape
    W = BRANCH_W
    ts = PROMPT_TILE
    grid = (B, S // ts)
    small = lambda width: _layer_spec((1, width), layer)
    in_specs = [
        pl.BlockSpec((1, ts, D), lambda b, s: (b, s, 0)),
        pl.BlockSpec((1, N_MEM, D), lambda b, s: (b, 0, 0)),
        small(D), small(D), small(D),
        _layer_spec((D, 2 * W), layer),
        _layer_spec(win.shape[1:], layer),
        _layer_spec(poolw.shape[1:], layer),
        small(W),
        _layer_spec((CONV_WIDTH, W), layer),
        small(W), small(W), small(W),
        _layer_spec(wbr.shape[1:], layer),
        _layer_spec((D, D), layer),
    ]
    per_layer = [(POOL_BUF, W), (CONV_BUF, W), (N_MEM * N_XHEADS, XHEAD_DIM), (N_MEM * N_XHEADS, XHEAD_DIM)]
    first = carried is None
    out_specs = [pl.BlockSpec((1, ts, D), lambda b, s: (b, s, 0))] + [
        _per_layer_spec((1,) + shape, lambda b, s: (b, 0, 0), layer, first) for shape in per_layer]
    out_shape = [jax.ShapeDtypeStruct((B, S, D), F32)] + [
        jax.ShapeDtypeStruct((DEPTH, B) + shape, F32) for shape in per_layer]
    operands = [x, mem, gpre, gpost, gmem, wkv, win, poolw, pscale, convw, convb, lng, lnb, wbr, wout]
    aliases = _carry_layers(operands, in_specs, carried, first_output=1)
    scratch = [
        pltpu.VMEM((POOL_HEAD + ts, W), F32),
        pltpu.VMEM((len(POOL_WINDOWS) - 1, POOL_HEAD + ts, POOL_GROUP_W), F32),
        pltpu.VMEM((CONV_HEAD + ts, W), F32),
        pltpu.VMEM((SUBLANES - 1, ts + CONV_HEAD - SUBLANES, W), F32),
        pltpu.VMEM((ts, N_BRANCH * D_MODEL), F32),
        pltpu.VMEM((W, N_MEM), BF16),
        pltpu.VMEM((N_MEM, W), BF16),
    ]
    body = _skip_carried(_prompt_kernel, len(operands) - len(aliases), len(aliases))
    return pl.pallas_call(
        functools.partial(body, ts=ts, layer=layer, first=first),
        grid=grid, in_specs=in_specs, out_specs=out_specs, out_shape=out_shape,
        scratch_shapes=scratch, name=f"prompt_layer{layer}", input_output_aliases=aliases,
        compiler_params=pltpu.CompilerParams(
            dimension_semantics=("arbitrary", "arbitrary"), vmem_limit_bytes=VMEM_LIMIT),
    )(*operands)


def _sample_layer(x, spool, sconv, ck, cv, layer, carried, gpre, gpost, win, poolw, pscale, convw, convb,
                  lng, lnb, wbr, wout):
    nt, nb, D = x.shape
    W = BRANCH_W
    nseq = SAMPLE_SEQS
    T = nseq * nt
    n_slabs = W // LANES
    small = lambda width: _layer_spec((1, width), layer)
    state_in = lambda rows: pl.BlockSpec((None, rows, nseq, W), lambda g: (layer, 0, g, 0))
    cache = pl.BlockSpec((None, nseq, N_MEM * N_XHEADS, XHEAD_DIM), lambda g: (layer, g, 0, 0))
    in_specs = [
        pl.BlockSpec((nt, nseq, D), lambda g: (0, g, 0)),
        state_in(POOL_BUF), state_in(CONV_BUF), cache, pl.BlockSpec(memory_space=pl.ANY),
        small(D), small(D),
        _layer_spec(win.shape[1:], layer),
        _layer_spec(poolw.shape[1:], layer),
        small(W),
        _layer_spec((CONV_WIDTH, W), layer),
        small(W), small(W), small(W),
        _layer_spec(wbr.shape[1:], layer),
        _layer_spec((D, D), layer),
    ]
    first = carried is None
    out_specs = [pl.BlockSpec((nt, nseq, D), lambda g: (0, g, 0))] + [
        _per_layer_spec((nseq, rows, W), lambda g: (g, 0, 0), layer, first) for rows in (POOL_BUF, CONV_BUF)]
    out_shape = [jax.ShapeDtypeStruct((nt, nb, D), F32)] + [
        jax.ShapeDtypeStruct((DEPTH, nb, rows, W), F32) for rows in (POOL_BUF, CONV_BUF)]
    operands = [x, spool, sconv, ck, cv, gpre, gpost, win, poolw, pscale, convw, convb, lng, lnb, wbr, wout]
    aliases = _carry_layers(operands, in_specs, carried, first_output=1)
    scratch = [
        pltpu.VMEM((n_slabs, (POOL_BUF + nt) * nseq, LANES), F32),
        pltpu.VMEM((n_slabs, (CONV_BUF + nt) * nseq, LANES), F32),
        pltpu.VMEM((N_XHEADS, T, XHEAD_DIM), F32),
        pltpu.VMEM((N_XHEADS, T, XHEAD_DIM), F32),
        pltpu.VMEM((nseq, N_MEM * N_XHEADS, XHEAD_DIM), F32),
        pltpu.SemaphoreType.DMA((V_CHUNKS,)),
    ]
    body = _skip_carried(_sample_kernel, len(operands) - len(aliases), len(aliases))
    return pl.pallas_call(
        functools.partial(body, nseq=nseq, layer=layer, first=first),
        grid=(nb // nseq,), in_specs=in_specs, out_specs=out_specs, out_shape=out_shape,
        scratch_shapes=scratch, name=f"sample_layer{layer}", input_output_aliases=aliases,
        compiler_params=pltpu.CompilerParams(
            dimension_semantics=("arbitrary",), vmem_limit_bytes=VMEM_LIMIT),
    )(*operands)


def kernel(x_prompt, x_sample, state_pool, state_conv, cache_mem_k, cache_mem_v, mem_prompt,
           norm_pre, norm_post, mem_norm, w_mem_kv, w_in, pool_w, pool_scale, conv_w, conv_b,
           conv_ln_g, conv_ln_b, w_branch, w_out):
    W = BRANCH_W
    nb, nt, D = x_sample.shape
    row = lambda p: p.reshape(DEPTH, 1, p.shape[-1])
    gpre, gpost, gmem = row(norm_pre), row(norm_post), row(mem_norm)
    pscale, convb, lng, lnb = row(pool_scale), row(conv_b), row(conv_ln_g), row(conv_ln_b)
    wkv, win, poolw = w_mem_kv.astype(BF16), w_in.astype(BF16), pool_w.astype(BF16)
    wbr, wout = w_branch.astype(BF16), w_out.astype(BF16)
    ck = cache_mem_k.reshape(DEPTH, nb, N_MEM * N_XHEADS, XHEAD_DIM)
    cv = cache_mem_v.reshape(DEPTH, nb, N_MEM * N_XHEADS, XHEAD_DIM)

    spool_t = state_pool.transpose(0, 2, 1, 3)
    sconv_t = state_conv.transpose(0, 2, 1, 3)
    xs = x_sample.transpose(1, 0, 2)

    xp = x_prompt
    prompt_new, sample_new = None, None
    for i in range(DEPTH):
        xp, *prompt_new = _prompt_layer(
            xp, mem_prompt, i, prompt_new, gpre, gpost, gmem, wkv, win, poolw, pscale, conv_w, convb,
            lng, lnb, wbr, wout)
        xs, *sample_new = _sample_layer(
            xs, spool_t, sconv_t, ck, cv, i, sample_new, gpre, gpost, win, poolw, pscale, conv_w,
            convb, lng, lnb, wbr, wout)
    pool_p, conv_p, mk_p, mv_p = prompt_new
    pool_s, conv_s = sample_new
    heads = lambda kv: kv.reshape(DEPTH, kv.shape[1], N_MEM, N_XHEADS, XHEAD_DIM)
    return (xp, xs.transpose(1, 0, 2), pool_p, conv_p, heads(mk_p), heads(mv_p), pool_s, conv_s)
```

```python
import functools
import math

import jax
import jax.numpy as jnp
from jax import lax
from jax.experimental import pallas as pl
from jax.experimental.pallas import tpu as pltpu

D_MODEL = 1024
DEPTH = 2
DEC_SEQ = 8
PAST_LEN = 16384
BRANCH_W = D_MODEL // 2
N_BRANCH = 3
POOL_WINDOWS = (2, 4, 8, 16)
POOL_GROUP_W = BRANCH_W // len(POOL_WINDOWS)
POOL_BUF = max(POOL_WINDOWS) - 1
CONV_WIDTH = 31
CONV_BUF = CONV_WIDTH - 1
N_MEM = 256
N_XHEADS = 4
XHEAD_DIM = BRANCH_W // N_XHEADS
N_IN_SLICES = 7
EPS = 1e-6

LANES = 128
SUBLANES = 8
POOL_PAD = 8
POOL_HEAD = POOL_PAD + 16
CONV_HEAD = 32
PROMPT_TILE = 256
SAMPLE_SEQS = 16
WIDE_STAGE_ROWS = 128
VMEM_LIMIT = 60 * 1024 * 1024
NEG_LOG2E = -1.0 / math.log(2.0)
MASKED_SCORE = -1e30

F32 = jnp.float32
BF16 = jnp.bfloat16


def _sigmoid(x):
    return 1.0 / (1.0 + jnp.exp2(x * NEG_LOG2E))


def _silu(x):
    return x * _sigmoid(x)


def _rmsnorm(x, g):
    return x * lax.rsqrt(jnp.mean(x * x, axis=-1, keepdims=True) + EPS) * g


def _dot(a, b):
    return jnp.dot(a, b, preferred_element_type=F32)


def _pool_finish(acc_parts, p_in, p_gate, cnt_parts, poolw_ref, pscale_ref):
    ys = []
    for g in range(len(POOL_WINDOWS)):
        c0 = g * POOL_GROUP_W
        mixed = acc_parts[g] / cnt_parts[g] - p_in[:, c0:c0 + POOL_GROUP_W]
        ys.append(_dot(mixed.astype(BF16), poolw_ref[g].astype(BF16)))
    y = jnp.concatenate(ys, axis=-1)
    return y * pscale_ref[...] * _silu(p_gate)


def _conv_finish(cv, c_gate, convb_ref, lng_ref, lnb_ref):
    cv = cv + convb_ref[...]
    mu = jnp.mean(cv, axis=-1, keepdims=True)
    d = cv - mu
    var = jnp.mean(d * d, axis=-1, keepdims=True)
    r = d * lax.rsqrt(var + EPS) * lng_ref[...] + lnb_ref[...]
    return _silu(r) * _silu(c_gate)


def _merge_out(x, h, branches, win_ref, wbr_ref, wout_ref, gpost_ref):
    merged = None
    base = N_IN_SLICES * BRANCH_W
    for n, br in enumerate(branches):
        proj = _dot(br.astype(BF16), wbr_ref[n])
        gate = _sigmoid(_dot(h, win_ref[:, base + n * D_MODEL:base + (n + 1) * D_MODEL]))
        term = gate * proj
        merged = term if merged is None else merged + term
    y = _dot(merged.astype(BF16), wout_ref[...])
    return x + _rmsnorm(y, gpost_ref[...])


def _own_layer(ref, layer, first):
    return ref.at[layer] if first else ref


def _zero_other_layers(ref, layer):
    for other in range(ref.shape[0]):
        if other != layer:
            ref[other] = jnp.zeros(ref.shape[1:], ref.dtype)


def _stage_and_cast(chunks, stage, sem):
    copies = [pltpu.make_async_copy(src, stage.at[i % 2], sem.at[i % 2])
              for i, (src, _) in enumerate(chunks)]
    copies[0].start()
    for i, (_, dst) in enumerate(chunks):
        if i + 1 < len(chunks):
            copies[i + 1].start()
        copies[i].wait()
        dst[...] = stage[i % 2].astype(BF16)


def _prompt_kernel(x_ref, mem_ref, gpre_ref, gpost_ref, gmem_ref, wkv_hbm, win_hbm, poolw_ref,
                   pscale_ref, convw_ref, convb_ref, lng_ref, lnb_ref, wbr_hbm, wout_hbm,
                   y_ref, npool_ref, nconv_ref, mk_ref, mv_ref, win_out, wbr_out, wout_out,
                   pool_ext, pool_tmp, conv_ext, conv_sh, gate_s, kt_s, v_s,
                   wkv_ref, win_ref, wbr_ref, wout_ref, stage_wide, stage_narrow, sem_wide, sem_narrow,
                   sem_out, *, ts, layer, first):
    W = BRANCH_W
    s = pl.program_id(1)
    first_step = jnp.logical_and(pl.program_id(0) == 0, s == 0)
    per_layer = (npool_ref, nconv_ref, mk_ref, mv_ref)
    npool_ref, nconv_ref, mk_ref, mv_ref = (_own_layer(r, layer, first) for r in per_layer)

    def hand_over():
        return [pltpu.make_async_copy(src, dst, sem_out.at[i]) for i, (src, dst) in enumerate(
            ((win_ref, win_out), (wbr_ref, wbr_out), (wout_ref, wout_out)))]

    @pl.when(first_step)
    def _():
        nrows = stage_narrow.shape[1]
        narrow = [(wkv_hbm.at[layer, pl.ds(r, nrows), :], wkv_ref.at[pl.ds(r, nrows), :])
                  for r in range(0, D_MODEL, nrows)]
        narrow += [(wbr_hbm.at[layer, n], wbr_ref.at[n]) for n in range(N_BRANCH)]
        narrow += [(wout_hbm.at[layer, pl.ds(r, nrows), :], wout_ref.at[pl.ds(r, nrows), :])
                   for r in range(0, D_MODEL, nrows)]
        _stage_and_cast(narrow, stage_narrow, sem_narrow)
        wrows = stage_wide.shape[1]
        wide = [(win_hbm.at[layer, pl.ds(r, wrows), :], win_ref.at[pl.ds(r, wrows), :])
                for r in range(0, D_MODEL, wrows)]
        _stage_and_cast(wide, stage_wide, sem_wide)
        for copy in hand_over():
            copy.start()

    @pl.when(s == 0)
    def _():
        if first:
            for r in per_layer:
                _zero_other_layers(r, layer)
        pool_ext[0:POOL_HEAD, :] = jnp.zeros((POOL_HEAD, W), F32)
        pool_tmp[:, 0:POOL_PAD, :] = jnp.zeros((pool_tmp.shape[0], POOL_PAD, POOL_GROUP_W), F32)
        conv_ext[0:CONV_HEAD, :] = jnp.zeros((CONV_HEAD, W), F32)
        hm = _rmsnorm(mem_ref[0], gmem_ref[...]).astype(BF16)
        kv = _dot(hm, wkv_ref[...])
        k = kv[:, :W]
        v = kv[:, W:]
        for hd in range(N_XHEADS):
            rows = pl.ds(hd, N_MEM, stride=N_XHEADS)
            mk_ref[0, rows, :] = k[:, hd * XHEAD_DIM:(hd + 1) * XHEAD_DIM]
            mv_ref[0, rows, :] = v[:, hd * XHEAD_DIM:(hd + 1) * XHEAD_DIM]
        kt_s[...] = k.T.astype(BF16)
        v_s[...] = v.astype(BF16)

    x = x_ref[0]
    h = _rmsnorm(x, gpre_ref[...]).astype(BF16)

    pb = _dot(h, win_ref[:, 2 * W:5 * W])
    u = pb[:, :W] * _sigmoid(pb[:, W:2 * W])
    c_gate = pb[:, 2 * W:]
    conv_ext[CONV_HEAD:CONV_HEAD + ts, :] = u
    for r in range(1, SUBLANES):
        conv_sh[r - 1, :, :] = conv_ext[r:r + ts + CONV_HEAD - SUBLANES, :]

    off = CONV_HEAD - CONV_BUF
    gate_base = N_IN_SLICES * W
    n_slabs = W // LANES
    gate_chunk = N_BRANCH * D_MODEL // n_slabs
    slabs = []
    for c in range(n_slabs):
        g0 = c * gate_chunk
        gate_s[:, g0:g0 + gate_chunk] = _sigmoid(
            _dot(h, win_ref[:, gate_base + g0:gate_base + g0 + gate_chunk]))
        c0 = c * LANES
        acc = None
        for k in range(CONV_WIDTH):
            a8, r = divmod(off + k, SUBLANES)
            if r == 0:
                rows = conv_ext[a8 * SUBLANES:a8 * SUBLANES + ts, c0:c0 + LANES]
            else:
                rows = conv_sh[r - 1, a8 * SUBLANES:a8 * SUBLANES + ts, c0:c0 + LANES]
            term = rows * convw_ref[k:k + 1, c0:c0 + LANES]
            acc = term if acc is None else acc + term
        slabs.append(acc)
    bconv = _conv_finish(jnp.concatenate(slabs, axis=-1), c_gate, convb_ref, lng_ref, lnb_ref)
    nconv_ref[0] = conv_ext[CONV_HEAD + ts - CONV_BUF:CONV_HEAD + ts, :]
    conv_ext[0:CONV_HEAD, :] = conv_ext[ts:ts + CONV_HEAD, :]

    pa = _dot(h, win_ref[:, 0:2 * W])
    p_in = pa[:, :W]
    p_gate = pa[:, W:]
    pool_ext[POOL_HEAD:POOL_HEAD + ts, :] = p_in
    pos1 = (lax.broadcasted_iota(jnp.int32, (ts, POOL_GROUP_W), 0) + s * ts + 1).astype(F32)
    accs, cnts = [], []
    for g, win in enumerate(POOL_WINDOWS):
        c0 = g * POOL_GROUP_W
        level, shift = None, 1
        while shift < win:
            last = 2 * shift == win
            lo = POOL_HEAD if last else SUBLANES
            n = POOL_HEAD + ts - lo
            if level is None:
                cur = (pool_ext[lo:lo + n, c0:c0 + POOL_GROUP_W]
                       + pool_ext[lo - shift:lo - shift + n, c0:c0 + POOL_GROUP_W])
            else:
                cur = pool_tmp[level, lo:lo + n, :] + pool_tmp[level, lo - shift:lo - shift + n, :]
            if last:
                accs.append(cur)
            else:
                level = 0 if level is None else level + 1
                pool_tmp[level, lo:lo + n, :] = cur
            shift *= 2
        cnts.append(jnp.minimum(pos1, float(win)))
    a = _pool_finish(accs, p_in, p_gate, cnts, poolw_ref, pscale_ref)
    npool_ref[0] = pool_ext[POOL_HEAD + ts - POOL_BUF:POOL_HEAD + ts, :]
    pool_ext[POOL_PAD:POOL_HEAD, :] = pool_ext[ts + POOL_PAD:ts + POOL_HEAD, :]

    pc = _dot(h, win_ref[:, 5 * W:7 * W])
    q = pc[:, :W]
    x_gate = pc[:, W:]
    scale = 1.0 / math.sqrt(XHEAD_DIM)
    outs = []
    for hd in range(N_XHEADS):
        c0 = hd * XHEAD_DIM
        sc = _dot((q[:, c0:c0 + XHEAD_DIM] * scale).astype(BF16), kt_s[c0:c0 + XHEAD_DIM, :])
        e = jnp.exp(sc - jnp.max(sc, axis=-1, keepdims=True))
        l = jnp.sum(e, axis=-1, keepdims=True)
        outs.append(_dot(e.astype(BF16), v_s[:, c0:c0 + XHEAD_DIM]) / l)
    cattn = jnp.concatenate(outs, axis=-1) * _silu(x_gate)

    merged = None
    for n, br in enumerate((a, bconv, cattn)):
        term = gate_s[:, n * D_MODEL:(n + 1) * D_MODEL] * _dot(br.astype(BF16), wbr_ref[n])
        merged = term if merged is None else merged + term
    y = _dot(merged.astype(BF16), wout_ref[...])
    y_ref[0] = x + _rmsnorm(y, gpost_ref[...])

    @pl.when(first_step)
    def _():
        for copy in hand_over():
            copy.wait()


def _sample_kernel(x_ref, spool_ref, sconv_ref, k_ref, v_ref, gpre_ref, gpost_ref, win_ref,
                   poolw_ref, pscale_ref, convw_ref, convb_ref, lng_ref, lnb_ref, wbr_ref, wout_ref,
                   y_ref, npool_ref, nconv_ref, pool_ext, conv_ext, q_s, o_s, v_buf, v_sem,
                   *, nseq, layer, first):
    W = BRANCH_W
    T = nseq * DEC_SEQ
    n_slabs = W // LANES
    step = pl.program_id(0)
    if first:
        _zero_other_layers(npool_ref, layer)
        _zero_other_layers(nconv_ref, layer)
    npool_ref = _own_layer(npool_ref, layer, first)
    nconv_ref = _own_layer(nconv_ref, layer, first)

    def v_copy(block):
        return pltpu.make_async_copy(v_ref.at[layer, pl.ds(block * nseq, nseq)], v_buf, v_sem)

    @pl.when(step == 0)
    def _():
        v_copy(0).start()

    x = x_ref[...].reshape(T, D_MODEL)
    h = _rmsnorm(x, gpre_ref[...]).astype(BF16)

    pa = _dot(h, win_ref[:, 0:2 * W])
    p_in = pa[:, :W]
    p_gate = pa[:, W:]
    pos1 = (lax.broadcasted_iota(jnp.int32, (DEC_SEQ, nseq, POOL_GROUP_W), 0) + PAST_LEN + 1).astype(F32)
    pos1 = pos1.reshape(T, POOL_GROUP_W)
    accs, cnts = [], []
    for g, win in enumerate(POOL_WINDOWS):
        c0 = g * POOL_GROUP_W
        pool_ext[g, 0:POOL_BUF * nseq, :] = spool_ref[:, :, c0:c0 + POOL_GROUP_W].reshape(
            POOL_BUF * nseq, POOL_GROUP_W)
        pool_ext[g, POOL_BUF * nseq:(POOL_BUF + DEC_SEQ) * nseq, :] = p_in[:, c0:c0 + POOL_GROUP_W]
        acc = None
        for i in range(win):
            r0 = (POOL_BUF - i) * nseq
            rows = pool_ext[g, r0:r0 + T, :]
            acc = rows if acc is None else acc + rows
        accs.append(acc)
        cnts.append(jnp.minimum(pos1, float(win)))
    a = _pool_finish(accs, p_in, p_gate, cnts, poolw_ref, pscale_ref)
    for sq in range(nseq):
        for g in range(n_slabs):
            npool_ref[sq, :, g * LANES:(g + 1) * LANES] = pool_ext[
                g, pl.ds(DEC_SEQ * nseq + sq, POOL_BUF, stride=nseq), :]

    pb = _dot(h, win_ref[:, 2 * W:5 * W])
    u = pb[:, :W] * _sigmoid(pb[:, W:2 * W])
    c_gate = pb[:, 2 * W:]
    slabs = []
    for c in range(n_slabs):
        c0 = c * LANES
        conv_ext[c, 0:CONV_BUF * nseq, :] = sconv_ref[:, :, c0:c0 + LANES].reshape(CONV_BUF * nseq, LANES)
        conv_ext[c, CONV_BUF * nseq:(CONV_BUF + DEC_SEQ) * nseq, :] = u[:, c0:c0 + LANES]
        acc = None
        for k in range(CONV_WIDTH):
            term = conv_ext[c, k * nseq:k * nseq + T, :] * convw_ref[k:k + 1, c0:c0 + LANES]
            acc = term if acc is None else acc + term
        slabs.append(acc)
    bconv = _conv_finish(jnp.concatenate(slabs, axis=-1), c_gate, convb_ref, lng_ref, lnb_ref)
    for sq in range(nseq):
        for c in range(n_slabs):
            nconv_ref[sq, :, c * LANES:(c + 1) * LANES] = conv_ext[
                c, pl.ds(DEC_SEQ * nseq + sq, CONV_BUF, stride=nseq), :]

    pc = _dot(h, win_ref[:, 5 * W:7 * W])
    x_gate = pc[:, W:]
    for hd in range(N_XHEADS):
        q_s[hd] = pc[:, hd * XHEAD_DIM:(hd + 1) * XHEAD_DIM]
    scale = 1.0 / math.sqrt(XHEAD_DIM)
    nq = N_XHEADS * DEC_SEQ
    nk = N_MEM * N_XHEADS
    q4 = jnp.stack([
        jnp.concatenate([q_s[hd, pl.ds(sq, DEC_SEQ, stride=nseq), :] for hd in range(N_XHEADS)], axis=0)
        for sq in range(nseq)]).astype(BF16)
    sc = jnp.einsum('gqd,gkd->gqk', q4, k_ref[...].astype(BF16), preferred_element_type=F32) * scale
    q_head = lax.broadcasted_iota(jnp.int32, (nq, nk), 0) // DEC_SEQ
    k_head = lax.broadcasted_iota(jnp.int32, (nq, nk), 1) % N_XHEADS
    sc = jnp.where((q_head == k_head)[None], sc, MASKED_SCORE)
    e = jnp.exp(sc - jnp.max(sc, axis=-1, keepdims=True))
    l = jnp.sum(e, axis=-1, keepdims=True)
    v_copy(step).wait()
    o = jnp.einsum('gqk,gkd->gqd', e.astype(BF16), v_buf[...].astype(BF16),
                   preferred_element_type=F32) / l
    for sq in range(nseq):
        for hd in range(N_XHEADS):
            o_s[hd, pl.ds(sq, DEC_SEQ, stride=nseq), :] = o[sq, hd * DEC_SEQ:(hd + 1) * DEC_SEQ, :]
    cattn = jnp.concatenate([o_s[hd] for hd in range(N_XHEADS)], axis=-1) * _silu(x_gate)

    y = _merge_out(x, h, (a, bconv, cattn), win_ref, wbr_ref, wout_ref, gpost_ref)
    y_ref[...] = y.reshape(DEC_SEQ, nseq, D_MODEL)

    @pl.when(step + 1 < pl.num_programs(0))
    def _():
        v_copy(step + 1).start()


def _layer_spec(shape, layer):
    zeros = (0,) * len(shape)
    return pl.BlockSpec((None,) + shape, lambda *_: (layer,) + zeros, pipeline_mode=pl.Buffered(1))


def _whole_spec(shape):
    zeros = (0,) * len(shape)
    return pl.BlockSpec(shape, lambda *_: zeros, pipeline_mode=pl.Buffered(1))


def _carry_layers(operands, in_specs, carried, first_output):
    aliases = {}
    for i, arr in enumerate(carried or ()):
        aliases[len(operands)] = first_output + i
        operands.append(arr)
        in_specs.append(pl.BlockSpec(memory_space=pl.ANY))
    return aliases


def _per_layer_spec(block, index_map, layer, first):
    if first:
        return pl.BlockSpec((DEPTH,) + block, lambda *i: (0,) + index_map(*i))
    return pl.BlockSpec((None,) + block, lambda *i: (layer,) + index_map(*i))


def _skip_carried(body, n_inputs, n_carried):
    def wrapped(*refs, **kw):
        return body(*refs[:n_inputs], *refs[n_inputs + n_carried:], **kw)
    return wrapped


def _prompt_layer(x, mem, layer, carried, gpre, gpost, gmem, wkv, win, poolw, pscale, convw, convb,
                  lng, lnb, wbr, wout):
    B, S, D = x.shape
    W = BRANCH_W
    ts = PROMPT_TILE
    grid = (B, S // ts)
    small = lambda width: _layer_spec((1, width), layer)
    in_hbm = pl.BlockSpec(memory_space=pl.ANY)
    in_specs = [
        pl.BlockSpec((1, ts, D), lambda b, s: (b, s, 0)),
        pl.BlockSpec((1, N_MEM, D), lambda b, s: (b, 0, 0)),
        small(D), small(D), small(D),
        in_hbm, in_hbm,
        _layer_spec(poolw.shape[1:], layer),
        small(W),
        _layer_spec((CONV_WIDTH, W), layer),
        small(W), small(W), small(W),
        in_hbm, in_hbm,
    ]
    per_layer = [(POOL_BUF, W), (CONV_BUF, W), (N_MEM * N_XHEADS, XHEAD_DIM), (N_MEM * N_XHEADS, XHEAD_DIM)]
    first = carried is None
    out_specs = [pl.BlockSpec((1, ts, D), lambda b, s: (b, s, 0))] + [
        _per_layer_spec((1,) + shape, lambda b, s: (b, 0, 0), layer, first) for shape in per_layer]
    out_shape = [jax.ShapeDtypeStruct((B, S, D), F32)] + [
        jax.ShapeDtypeStruct((DEPTH, B) + shape, F32) for shape in per_layer]
    handed = [win.shape[1:], wbr.shape[1:], wout.shape[1:]]
    out_specs += [pl.BlockSpec(memory_space=pl.ANY)] * len(handed)
    out_shape += [jax.ShapeDtypeStruct(shape, BF16) for shape in handed]
    operands = [x, mem, gpre, gpost, gmem, wkv, win, poolw, pscale, convw, convb, lng, lnb, wbr, wout]
    aliases = _carry_layers(operands, in_specs, carried, first_output=1)
    scratch = [
        pltpu.VMEM((POOL_HEAD + ts, W), F32),
        pltpu.VMEM((len(POOL_WINDOWS) - 1, POOL_HEAD + ts, POOL_GROUP_W), F32),
        pltpu.VMEM((CONV_HEAD + ts, W), F32),
        pltpu.VMEM((SUBLANES - 1, ts + CONV_HEAD - SUBLANES, W), F32),
        pltpu.VMEM((ts, N_BRANCH * D_MODEL), F32),
        pltpu.VMEM((W, N_MEM), BF16),
        pltpu.VMEM((N_MEM, W), BF16),
        pltpu.VMEM(wkv.shape[1:], BF16),
        pltpu.VMEM(win.shape[1:], BF16),
        pltpu.VMEM(wbr.shape[1:], BF16),
        pltpu.VMEM(wout.shape[1:], BF16),
        pltpu.VMEM((2, WIDE_STAGE_ROWS, win.shape[-1]), F32),
        pltpu.VMEM((2, W, D), F32),
        pltpu.SemaphoreType.DMA((2,)),
        pltpu.SemaphoreType.DMA((2,)),
        pltpu.SemaphoreType.DMA((len(handed),)),
    ]
    body = _skip_carried(_prompt_kernel, len(operands) - len(aliases), len(aliases))
    return pl.pallas_call(
        functools.partial(body, ts=ts, layer=layer, first=first),
        grid=grid, in_specs=in_specs, out_specs=out_specs, out_shape=out_shape,
        scratch_shapes=scratch, name=f"prompt_layer{layer}", input_output_aliases=aliases,
        compiler_params=pltpu.CompilerParams(
            dimension_semantics=("arbitrary", "arbitrary"), vmem_limit_bytes=VMEM_LIMIT),
    )(*operands)


def _sample_layer(x, spool, sconv, ck, cv, layer, carried, gpre, gpost, win, poolw, pscale, convw, convb,
                  lng, lnb, wbr, wout):
    nt, nb, D = x.shape
    W = BRANCH_W
    nseq = SAMPLE_SEQS
    T = nseq * nt
    n_slabs = W // LANES
    small = lambda width: _layer_spec((1, width), layer)
    state_in = lambda rows: pl.BlockSpec((None, rows, nseq, W), lambda g: (layer, 0, g, 0))
    cache = pl.BlockSpec((None, nseq, N_MEM * N_XHEADS, XHEAD_DIM), lambda g: (layer, g, 0, 0))
    in_specs = [
        pl.BlockSpec((nt, nseq, D), lambda g: (0, g, 0)),
        state_in(POOL_BUF), state_in(CONV_BUF), cache, pl.BlockSpec(memory_space=pl.ANY),
        small(D), small(D),
        _whole_spec(win.shape),
        _layer_spec(poolw.shape[1:], layer),
        small(W),
        _layer_spec((CONV_WIDTH, W), layer),
        small(W), small(W), small(W),
        _whole_spec(wbr.shape),
        _whole_spec(wout.shape),
    ]
    first = carried is None
    out_specs = [pl.BlockSpec((nt, nseq, D), lambda g: (0, g, 0))] + [
        _per_layer_spec((nseq, rows, W), lambda g: (g, 0, 0), layer, first) for rows in (POOL_BUF, CONV_BUF)]
    out_shape = [jax.ShapeDtypeStruct((nt, nb, D), F32)] + [
        jax.ShapeDtypeStruct((DEPTH, nb, rows, W), F32) for rows in (POOL_BUF, CONV_BUF)]
    operands = [x, spool, sconv, ck, cv, gpre, gpost, win, poolw, pscale, convw, convb, lng, lnb, wbr, wout]
    aliases = _carry_layers(operands, in_specs, carried, first_output=1)
    scratch = [
        pltpu.VMEM((n_slabs, (POOL_BUF + nt) * nseq, LANES), F32),
        pltpu.VMEM((n_slabs, (CONV_BUF + nt) * nseq, LANES), F32),
        pltpu.VMEM((N_XHEADS, T, XHEAD_DIM), F32),
        pltpu.VMEM((N_XHEADS, T, XHEAD_DIM), F32),
        pltpu.VMEM((nseq, N_MEM * N_XHEADS, XHEAD_DIM), F32),
        pltpu.SemaphoreType.DMA(()),
    ]
    body = _skip_carried(_sample_kernel, len(operands) - len(aliases), len(aliases))
    return pl.pallas_call(
        functools.partial(body, nseq=nseq, layer=layer, first=first),
        grid=(nb // nseq,), in_specs=in_specs, out_specs=out_specs, out_shape=out_shape,
        scratch_shapes=scratch, name=f"sample_layer{layer}", input_output_aliases=aliases,
        compiler_params=pltpu.CompilerParams(
            dimension_semantics=("arbitrary",), vmem_limit_bytes=VMEM_LIMIT),
    )(*operands)


def kernel(x_prompt, x_sample, state_pool, state_conv, cache_mem_k, cache_mem_v, mem_prompt,
           norm_pre, norm_post, mem_norm, w_mem_kv, w_in, pool_w, pool_scale, conv_w, conv_b,
           conv_ln_g, conv_ln_b, w_branch, w_out):
    W = BRANCH_W
    nb, nt, D = x_sample.shape
    row = lambda p: p.reshape(DEPTH, 1, p.shape[-1])
    gpre, gpost, gmem = row(norm_pre), row(norm_post), row(mem_norm)
    pscale, convb, lng, lnb = row(pool_scale), row(conv_b), row(conv_ln_g), row(conv_ln_b)
    ck = cache_mem_k.reshape(DEPTH, nb, N_MEM * N_XHEADS, XHEAD_DIM)
    cv = cache_mem_v.reshape(DEPTH, nb, N_MEM * N_XHEADS, XHEAD_DIM)

    spool_t = state_pool.transpose(0, 2, 1, 3)
    sconv_t = state_conv.transpose(0, 2, 1, 3)
    xs = x_sample.transpose(1, 0, 2)

    xp = x_prompt
    prompt_new, sample_new = None, None
    for i in range(DEPTH):
        xp, *prompt_new = _prompt_layer(
            xp, mem_prompt, i, prompt_new, gpre, gpost, gmem, w_mem_kv, w_in, pool_w, pscale, conv_w,
            convb, lng, lnb, w_branch, w_out)
        win, wbr, wout = prompt_new[-3:]
        prompt_new = prompt_new[:-3]
        xs, *sample_new = _sample_layer(
            xs, spool_t, sconv_t, ck, cv, i, sample_new, gpre, gpost, win, pool_w, pscale, conv_w,
            convb, lng, lnb, wbr, wout)
    pool_p, conv_p, mk_p, mv_p = prompt_new
    pool_s, conv_s = sample_new
    heads = lambda kv: kv.reshape(DEPTH, kv.shape[1], N_MEM, N_XHEADS, XHEAD_DIM)
    return (xp, xs.transpose(1, 0, 2), pool_p, conv_p, heads(mk_p), heads(mv_p), pool_s, conv_s)
```

```python
import functools
import math

import jax
import jax.numpy as jnp
from jax import lax
from jax.experimental import pallas as pl
from jax.experimental.pallas import tpu as pltpu

D_MODEL = 1024
DEPTH = 2
DEC_SEQ = 8
PAST_LEN = 16384
BRANCH_W = D_MODEL // 2
N_BRANCH = 3
POOL_WINDOWS = (2, 4, 8, 16)
POOL_GROUP_W = BRANCH_W // len(POOL_WINDOWS)
POOL_BUF = max(POOL_WINDOWS) - 1
CONV_WIDTH = 31
CONV_BUF = CONV_WIDTH - 1
N_MEM = 256
N_XHEADS = 4
XHEAD_DIM = BRANCH_W // N_XHEADS
N_IN_SLICES = 7
EPS = 1e-6

LANES = 128
SUBLANES = 8
POOL_PAD = 8
POOL_HEAD = POOL_PAD + 16
CONV_HEAD = 32
PROMPT_TILE = 256
SAMPLE_SEQS = 16
WIDE_STAGE_ROWS = 128
VMEM_LIMIT = 60 * 1024 * 1024
NEG_LOG2E = -1.0 / math.log(2.0)
MASKED_SCORE = -1e30

F32 = jnp.float32
BF16 = jnp.bfloat16


def _sigmoid(x):
    return 1.0 / (1.0 + jnp.exp2(x * NEG_LOG2E))


def _silu(x):
    return x * _sigmoid(x)


def _rmsnorm(x, g):
    return x * lax.rsqrt(jnp.mean(x * x, axis=-1, keepdims=True) + EPS) * g


def _dot(a, b):
    return jnp.dot(a, b, preferred_element_type=F32)


def _pool_finish(acc_parts, p_in, p_gate, cnt_parts, poolw_ref, pscale_ref):
    ys = []
    for g in range(len(POOL_WINDOWS)):
        c0 = g * POOL_GROUP_W
        mixed = acc_parts[g] / cnt_parts[g] - p_in[:, c0:c0 + POOL_GROUP_W]
        ys.append(_dot(mixed.astype(BF16), poolw_ref[g].astype(BF16)))
    y = jnp.concatenate(ys, axis=-1)
    return y * pscale_ref[...] * _silu(p_gate)


def _conv_finish(cv, c_gate, convb_ref, lng_ref, lnb_ref):
    cv = cv + convb_ref[...]
    mu = jnp.mean(cv, axis=-1, keepdims=True)
    d = cv - mu
    var = jnp.mean(d * d, axis=-1, keepdims=True)
    r = d * lax.rsqrt(var + EPS) * lng_ref[...] + lnb_ref[...]
    return _silu(r) * _silu(c_gate)


def _merge_out(x, h, branches, win_ref, wbr_ref, wout_ref, gpost_ref):
    merged = None
    base = N_IN_SLICES * BRANCH_W
    for n, br in enumerate(branches):
        proj = _dot(br.astype(BF16), wbr_ref[n])
        gate = _sigmoid(_dot(h, win_ref[:, base + n * D_MODEL:base + (n + 1) * D_MODEL]))
        term = gate * proj
        merged = term if merged is None else merged + term
    y = _dot(merged.astype(BF16), wout_ref[...])
    return x + _rmsnorm(y, gpost_ref[...])


def _own_layer(ref, layer, first):
    return ref.at[layer] if first else ref


def _zero_other_layers(ref, layer):
    for other in range(ref.shape[0]):
        if other != layer:
            ref[other] = jnp.zeros(ref.shape[1:], ref.dtype)


def _stage_and_cast(chunks, stage, sem):
    copies = [pltpu.make_async_copy(src, stage.at[i % 2], sem.at[i % 2])
              for i, (src, _) in enumerate(chunks)]
    copies[0].start()
    for i, (_, dst) in enumerate(chunks):
        if i + 1 < len(chunks):
            copies[i + 1].start()
        copies[i].wait()
        dst[...] = stage[i % 2].astype(BF16)


def _prompt_kernel(x_ref, mem_ref, gpre_ref, gpost_ref, gmem_ref, wkv_hbm, win_hbm, poolw_ref,
                   pscale_ref, convw_ref, convb_ref, lng_ref, lnb_ref, wbr_hbm, wout_hbm,
                   y_ref, npool_ref, nconv_ref, mk_ref, mv_ref, win_out, wbr_out, wout_out,
                   pool_ext, pool_tmp, conv_ext, conv_sh, gate_s, kt_s, v_s,
                   wkv_ref, win_ref, wbr_ref, wout_ref, stage_wide, stage_narrow, sem_wide, sem_narrow,
                   sem_out, *, ts, layer, first):
    W = BRANCH_W
    s = pl.program_id(1)
    first_step = jnp.logical_and(pl.program_id(0) == 0, s == 0)
    per_layer = (npool_ref, nconv_ref, mk_ref, mv_ref)
    npool_ref, nconv_ref, mk_ref, mv_ref = (_own_layer(r, layer, first) for r in per_layer)

    def hand_over():
        return [pltpu.make_async_copy(src, dst, sem_out.at[i]) for i, (src, dst) in enumerate(
            ((win_ref, win_out), (wbr_ref, wbr_out), (wout_ref, wout_out)))]

    @pl.when(first_step)
    def _():
        nrows = stage_narrow.shape[1]
        narrow = [(wkv_hbm.at[layer, pl.ds(r, nrows), :], wkv_ref.at[pl.ds(r, nrows), :])
                  for r in range(0, D_MODEL, nrows)]
        narrow += [(wbr_hbm.at[layer, n], wbr_ref.at[n]) for n in range(N_BRANCH)]
        narrow += [(wout_hbm.at[layer, pl.ds(r, nrows), :], wout_ref.at[pl.ds(r, nrows), :])
                   for r in range(0, D_MODEL, nrows)]
        _stage_and_cast(narrow, stage_narrow, sem_narrow)
        wrows = stage_wide.shape[1]
        wide = [(win_hbm.at[layer, pl.ds(r, wrows), :], win_ref.at[pl.ds(r, wrows), :])
                for r in range(0, D_MODEL, wrows)]
        _stage_and_cast(wide, stage_wide, sem_wide)
        for copy in hand_over():
            copy.start()

    @pl.when(s == 0)
    def _():
        if first:
            for r in per_layer:
                _zero_other_layers(r, layer)
        pool_ext[0:POOL_HEAD, :] = jnp.zeros((POOL_HEAD, W), F32)
        pool_tmp[:, 0:POOL_PAD, :] = jnp.zeros((pool_tmp.shape[0], POOL_PAD, POOL_GROUP_W), F32)
        conv_ext[0:CONV_HEAD, :] = jnp.zeros((CONV_HEAD, W), F32)
        hm = _rmsnorm(mem_ref[0], gmem_ref[...]).astype(BF16)
        kv = _dot(hm, wkv_ref[...])
        k = kv[:, :W]
        v = kv[:, W:]
        for hd in range(N_XHEADS):
            rows = pl.ds(hd, N_MEM, stride=N_XHEADS)
            mk_ref[0, rows, :] = k[:, hd * XHEAD_DIM:(hd + 1) * XHEAD_DIM]
            mv_ref[0, rows, :] = v[:, hd * XHEAD_DIM:(hd + 1) * XHEAD_DIM]
        kt_s[...] = k.T.astype(BF16)
        v_s[...] = v.astype(BF16)

    x = x_ref[0]
    h = _rmsnorm(x, gpre_ref[...]).astype(BF16)

    pb = _dot(h, win_ref[:, 2 * W:5 * W])
    u = pb[:, :W] * _sigmoid(pb[:, W:2 * W])
    c_gate = pb[:, 2 * W:]
    conv_ext[CONV_HEAD:CONV_HEAD + ts, :] = u
    for r in range(1, SUBLANES):
        conv_sh[r - 1, :, :] = conv_ext[r:r + ts + CONV_HEAD - SUBLANES, :]

    off = CONV_HEAD - CONV_BUF
    gate_base = N_IN_SLICES * W
    n_slabs = W // LANES
    gate_chunk = N_BRANCH * D_MODEL // n_slabs
    slabs = []
    for c in range(n_slabs):
        g0 = c * gate_chunk
        gate_s[:, g0:g0 + gate_chunk] = _sigmoid(
            _dot(h, win_ref[:, gate_base + g0:gate_base + g0 + gate_chunk]))
        c0 = c * LANES
        acc = None
        for k in range(CONV_WIDTH):
            a8, r = divmod(off + k, SUBLANES)
            if r == 0:
                rows = conv_ext[a8 * SUBLANES:a8 * SUBLANES + ts, c0:c0 + LANES]
            else:
                rows = conv_sh[r - 1, a8 * SUBLANES:a8 * SUBLANES + ts, c0:c0 + LANES]
            term = rows * convw_ref[k:k + 1, c0:c0 + LANES]
            acc = term if acc is None else acc + term
        slabs.append(acc)
    bconv = _conv_finish(jnp.concatenate(slabs, axis=-1), c_gate, convb_ref, lng_ref, lnb_ref)
    nconv_ref[0] = conv_ext[CONV_HEAD + ts - CONV_BUF:CONV_HEAD + ts, :]
    conv_ext[0:CONV_HEAD, :] = conv_ext[ts:ts + CONV_HEAD, :]

    pa = _dot(h, win_ref[:, 0:2 * W])
    p_in = pa[:, :W]
    p_gate = pa[:, W:]
    pool_ext[POOL_HEAD:POOL_HEAD + ts, :] = p_in
    pos1 = (lax.broadcasted_iota(jnp.int32, (ts, POOL_GROUP_W), 0) + s * ts + 1).astype(F32)
    accs, cnts = [], []
    for g, win in enumerate(POOL_WINDOWS):
        c0 = g * POOL_GROUP_W
        level, shift = None, 1
        while shift < win:
            last = 2 * shift == win
            lo = POOL_HEAD if last else SUBLANES
            n = POOL_HEAD + ts - lo
            if level is None:
                cur = (pool_ext[lo:lo + n, c0:c0 + POOL_GROUP_W]
                       + pool_ext[lo - shift:lo - shift + n, c0:c0 + POOL_GROUP_W])
            else:
                cur = pool_tmp[level, lo:lo + n, :] + pool_tmp[level, lo - shift:lo - shift + n, :]
            if last:
                accs.append(cur)
            else:
                level = 0 if level is None else level + 1
                pool_tmp[level, lo:lo + n, :] = cur
            shift *= 2
        cnts.append(jnp.minimum(pos1, float(win)))
    a = _pool_finish(accs, p_in, p_gate, cnts, poolw_ref, pscale_ref)
    npool_ref[0] = pool_ext[POOL_HEAD + ts - POOL_BUF:POOL_HEAD + ts, :]
    pool_ext[POOL_PAD:POOL_HEAD, :] = pool_ext[ts + POOL_PAD:ts + POOL_HEAD, :]

    pc = _dot(h, win_ref[:, 5 * W:7 * W])
    q = pc[:, :W]
    x_gate = pc[:, W:]
    scale = 1.0 / math.sqrt(XHEAD_DIM)
    outs = []
    for hd in range(N_XHEADS):
        c0 = hd * XHEAD_DIM
        sc = _dot((q[:, c0:c0 + XHEAD_DIM] * scale).astype(BF16), kt_s[c0:c0 + XHEAD_DIM, :])
        e = jnp.exp(sc - jnp.max(sc, axis=-1, keepdims=True))
        l = jnp.sum(e, axis=-1, keepdims=True)
        outs.append(_dot(e.astype(BF16), v_s[:, c0:c0 + XHEAD_DIM]) / l)
    cattn = jnp.concatenate(outs, axis=-1) * _silu(x_gate)

    merged = None
    for n, br in enumerate((a, bconv, cattn)):
        term = gate_s[:, n * D_MODEL:(n + 1) * D_MODEL] * _dot(br.astype(BF16), wbr_ref[n])
        merged = term if merged is None else merged + term
    y = _dot(merged.astype(BF16), wout_ref[...])
    y_ref[0] = x + _rmsnorm(y, gpost_ref[...])

    @pl.when(first_step)
    def _():
        for copy in hand_over():
            copy.wait()


def _sample_kernel(x_ref, spool_ref, sconv_ref, k_ref, v_ref, gpre_ref, gpost_ref, win_ref,
                   poolw_ref, pscale_ref, convw_ref, convb_ref, lng_ref, lnb_ref, wbr_ref, wout_ref,
                   y_ref, npool_ref, nconv_ref, pool_ext, conv_ext, q_s, o_s, xy_s, v_buf, v_sem,
                   *, nseq, layer, first):
    W = BRANCH_W
    T = nseq * DEC_SEQ
    n_slabs = W // LANES
    step = pl.program_id(0)
    if first:
        _zero_other_layers(npool_ref, layer)
        _zero_other_layers(nconv_ref, layer)
    npool_ref = _own_layer(npool_ref, layer, first)
    nconv_ref = _own_layer(nconv_ref, layer, first)

    def v_copy(block):
        return pltpu.make_async_copy(v_ref.at[layer, pl.ds(block * nseq, nseq)], v_buf, v_sem)

    @pl.when(step == 0)
    def _():
        v_copy(0).start()

    x_seq = x_ref[...].reshape(T, D_MODEL)
    for c in range(D_MODEL // LANES):
        xy_s[c] = x_seq[:, c * LANES:(c + 1) * LANES]
    x = jnp.concatenate([
        jnp.concatenate([xy_s[c, pl.ds(t, nseq, stride=DEC_SEQ), :] for c in range(D_MODEL // LANES)], axis=-1)
        for t in range(DEC_SEQ)], axis=0)
    h = _rmsnorm(x, gpre_ref[...]).astype(BF16)

    pa = _dot(h, win_ref[:, 0:2 * W])
    p_in = pa[:, :W]
    p_gate = pa[:, W:]
    pos1 = (lax.broadcasted_iota(jnp.int32, (DEC_SEQ, nseq, POOL_GROUP_W), 0) + PAST_LEN + 1).astype(F32)
    pos1 = pos1.reshape(T, POOL_GROUP_W)
    accs, cnts = [], []
    for g, win in enumerate(POOL_WINDOWS):
        c0 = g * POOL_GROUP_W
        pool_ext[g, 0:POOL_BUF * nseq, :] = spool_ref[:, :, c0:c0 + POOL_GROUP_W].reshape(
            POOL_BUF * nseq, POOL_GROUP_W)
        pool_ext[g, POOL_BUF * nseq:(POOL_BUF + DEC_SEQ) * nseq, :] = p_in[:, c0:c0 + POOL_GROUP_W]
        acc = None
        for i in range(win):
            r0 = (POOL_BUF - i) * nseq
            rows = pool_ext[g, r0:r0 + T, :]
            acc = rows if acc is None else acc + rows
        accs.append(acc)
        cnts.append(jnp.minimum(pos1, float(win)))
    a = _pool_finish(accs, p_in, p_gate, cnts, poolw_ref, pscale_ref)
    for g in range(n_slabs):
        npool_ref[:, :, g * LANES:(g + 1) * LANES] = pool_ext[
            g, DEC_SEQ * nseq:(DEC_SEQ + POOL_BUF) * nseq, :].reshape(POOL_BUF, nseq, LANES)

    pb = _dot(h, win_ref[:, 2 * W:5 * W])
    u = pb[:, :W] * _sigmoid(pb[:, W:2 * W])
    c_gate = pb[:, 2 * W:]
    slabs = []
    for c in range(n_slabs):
        c0 = c * LANES
        conv_ext[c, 0:CONV_BUF * nseq, :] = sconv_ref[:, :, c0:c0 + LANES].reshape(CONV_BUF * nseq, LANES)
        conv_ext[c, CONV_BUF * nseq:(CONV_BUF + DEC_SEQ) * nseq, :] = u[:, c0:c0 + LANES]
        acc = None
        for k in range(CONV_WIDTH):
            term = conv_ext[c, k * nseq:k * nseq + T, :] * convw_ref[k:k + 1, c0:c0 + LANES]
            acc = term if acc is None else acc + term
        slabs.append(acc)
    bconv = _conv_finish(jnp.concatenate(slabs, axis=-1), c_gate, convb_ref, lng_ref, lnb_ref)
    for c in range(n_slabs):
        nconv_ref[:, :, c * LANES:(c + 1) * LANES] = conv_ext[
            c, DEC_SEQ * nseq:(DEC_SEQ + CONV_BUF) * nseq, :].reshape(CONV_BUF, nseq, LANES)

    pc = _dot(h, win_ref[:, 5 * W:7 * W])
    x_gate = pc[:, W:]
    for hd in range(N_XHEADS):
        q_s[hd] = pc[:, hd * XHEAD_DIM:(hd + 1) * XHEAD_DIM]
    scale = 1.0 / math.sqrt(XHEAD_DIM)
    nq = N_XHEADS * DEC_SEQ
    nk = N_MEM * N_XHEADS
    q4 = jnp.stack([
        jnp.concatenate([q_s[hd, pl.ds(sq, DEC_SEQ, stride=nseq), :] for hd in range(N_XHEADS)], axis=0)
        for sq in range(nseq)]).astype(BF16)
    sc = jnp.einsum('gqd,gkd->gqk', q4, k_ref[...].astype(BF16), preferred_element_type=F32) * scale
    q_head = lax.broadcasted_iota(jnp.int32, (nq, nk), 0) // DEC_SEQ
    k_head = lax.broadcasted_iota(jnp.int32, (nq, nk), 1) % N_XHEADS
    sc = jnp.where((q_head == k_head)[None], sc, MASKED_SCORE)
    e = jnp.exp(sc - jnp.max(sc, axis=-1, keepdims=True))
    l = jnp.sum(e, axis=-1, keepdims=True)
    v_copy(step).wait()
    o = jnp.einsum('gqk,gkd->gqd', e.astype(BF16), v_buf[...].astype(BF16),
                   preferred_element_type=F32) / l
    for sq in range(nseq):
        for hd in range(N_XHEADS):
            o_s[hd, pl.ds(sq, DEC_SEQ, stride=nseq), :] = o[sq, hd * DEC_SEQ:(hd + 1) * DEC_SEQ, :]
    cattn = jnp.concatenate([o_s[hd] for hd in range(N_XHEADS)], axis=-1) * _silu(x_gate)

    y = _merge_out(x, h, (a, bconv, cattn), win_ref, wbr_ref, wout_ref, gpost_ref)
    for c in range(D_MODEL // LANES):
        xy_s[c] = y[:, c * LANES:(c + 1) * LANES]
    for sq in range(nseq):
        y_ref[sq] = jnp.concatenate(
            [xy_s[c, pl.ds(sq, DEC_SEQ, stride=nseq), :] for c in range(D_MODEL // LANES)], axis=-1)

    @pl.when(step + 1 < pl.num_programs(0))
    def _():
        v_copy(step + 1).start()


def _layer_spec(shape, layer):
    zeros = (0,) * len(shape)
    return pl.BlockSpec((None,) + shape, lambda *_: (layer,) + zeros, pipeline_mode=pl.Buffered(1))


def _whole_spec(shape):
    zeros = (0,) * len(shape)
    return pl.BlockSpec(shape, lambda *_: zeros, pipeline_mode=pl.Buffered(1))


def _carry_layers(operands, in_specs, carried, first_output):
    aliases = {}
    for i, arr in enumerate(carried or ()):
        aliases[len(operands)] = first_output + i
        operands.append(arr)
        in_specs.append(pl.BlockSpec(memory_space=pl.ANY))
    return aliases


def _per_layer_spec(block, index_map, layer, first):
    if first:
        return pl.BlockSpec((DEPTH,) + block, lambda *i: (0,) + index_map(*i))
    return pl.BlockSpec((None,) + block, lambda *i: (layer,) + index_map(*i))


def _skip_carried(body, n_inputs, n_carried):
    def wrapped(*refs, **kw):
        return body(*refs[:n_inputs], *refs[n_inputs + n_carried:], **kw)
    return wrapped


def _prompt_layer(x, mem, layer, carried, gpre, gpost, gmem, wkv, win, poolw, pscale, convw, convb,
                  lng, lnb, wbr, wout):
    B, S, D = x.shape
    W = BRANCH_W
    ts = PROMPT_TILE
    grid = (B, S // ts)
    small = lambda width: _layer_spec((1, width), layer)
    in_hbm = pl.BlockSpec(memory_space=pl.ANY)
    in_specs = [
        pl.BlockSpec((1, ts, D), lambda b, s: (b, s, 0)),
        pl.BlockSpec((1, N_MEM, D), lambda b, s: (b, 0, 0)),
        small(D), small(D), small(D),
        in_hbm, in_hbm,
        _layer_spec(poolw.shape[1:], layer),
        small(W),
        _layer_spec((CONV_WIDTH, W), layer),
        small(W), small(W), small(W),
        in_hbm, in_hbm,
    ]
    per_layer = [(POOL_BUF, W), (CONV_BUF, W), (N_MEM * N_XHEADS, XHEAD_DIM), (N_MEM * N_XHEADS, XHEAD_DIM)]
    first = carried is None
    out_specs = [pl.BlockSpec((1, ts, D), lambda b, s: (b, s, 0))] + [
        _per_layer_spec((1,) + shape, lambda b, s: (b, 0, 0), layer, first) for shape in per_layer]
    out_shape = [jax.ShapeDtypeStruct((B, S, D), F32)] + [
        jax.ShapeDtypeStruct((DEPTH, B) + shape, F32) for shape in per_layer]
    handed = [win.shape[1:], wbr.shape[1:], wout.shape[1:]]
    out_specs += [pl.BlockSpec(memory_space=pl.ANY)] * len(handed)
    out_shape += [jax.ShapeDtypeStruct(shape, BF16) for shape in handed]
    operands = [x, mem, gpre, gpost, gmem, wkv, win, poolw, pscale, convw, convb, lng, lnb, wbr, wout]
    aliases = _carry_layers(operands, in_specs, carried, first_output=1)
    scratch = [
        pltpu.VMEM((POOL_HEAD + ts, W), F32),
        pltpu.VMEM((len(POOL_WINDOWS) - 1, POOL_HEAD + ts, POOL_GROUP_W), F32),
        pltpu.VMEM((CONV_HEAD + ts, W), F32),
        pltpu.VMEM((SUBLANES - 1, ts + CONV_HEAD - SUBLANES, W), F32),
        pltpu.VMEM((ts, N_BRANCH * D_MODEL), F32),
        pltpu.VMEM((W, N_MEM), BF16),
        pltpu.VMEM((N_MEM, W), BF16),
        pltpu.VMEM(wkv.shape[1:], BF16),
        pltpu.VMEM(win.shape[1:], BF16),
        pltpu.VMEM(wbr.shape[1:], BF16),
        pltpu.VMEM(wout.shape[1:], BF16),
        pltpu.VMEM((2, WIDE_STAGE_ROWS, win.shape[-1]), F32),
        pltpu.VMEM((2, W, D), F32),
        pltpu.SemaphoreType.DMA((2,)),
        pltpu.SemaphoreType.DMA((2,)),
        pltpu.SemaphoreType.DMA((len(handed),)),
    ]
    body = _skip_carried(_prompt_kernel, len(operands) - len(aliases), len(aliases))
    return pl.pallas_call(
        functools.partial(body, ts=ts, layer=layer, first=first),
        grid=grid, in_specs=in_specs, out_specs=out_specs, out_shape=out_shape,
        scratch_shapes=scratch, name=f"prompt_layer{layer}", input_output_aliases=aliases,
        compiler_params=pltpu.CompilerParams(
            dimension_semantics=("arbitrary", "arbitrary"), vmem_limit_bytes=VMEM_LIMIT),
    )(*operands)


def _sample_layer(x, spool, sconv, ck, cv, layer, carried, gpre, gpost, win, poolw, pscale, convw, convb,
                  lng, lnb, wbr, wout):
    nb, nt, D = x.shape
    W = BRANCH_W
    nseq = SAMPLE_SEQS
    T = nseq * nt
    n_slabs = W // LANES
    small = lambda width: _layer_spec((1, width), layer)
    state_in = lambda rows: pl.BlockSpec((None, rows, nseq, W), lambda g: (layer, 0, g, 0))
    cache = pl.BlockSpec((None, nseq, N_MEM * N_XHEADS, XHEAD_DIM), lambda g: (layer, g, 0, 0))
    in_specs = [
        pl.BlockSpec((nseq, nt, D), lambda g: (g, 0, 0)),
        state_in(POOL_BUF), state_in(CONV_BUF), cache, pl.BlockSpec(memory_space=pl.ANY),
        small(D), small(D),
        _whole_spec(win.shape),
        _layer_spec(poolw.shape[1:], layer),
        small(W),
        _layer_spec((CONV_WIDTH, W), layer),
        small(W), small(W), small(W),
        _whole_spec(wbr.shape),
        _whole_spec(wout.shape),
    ]
    first = carried is None
    out_specs = [pl.BlockSpec((nseq, nt, D), lambda g: (g, 0, 0))] + [
        _per_layer_spec((rows, nseq, W), lambda g: (0, g, 0), layer, first) for rows in (POOL_BUF, CONV_BUF)]
    out_shape = [jax.ShapeDtypeStruct((nb, nt, D), F32)] + [
        jax.ShapeDtypeStruct((DEPTH, rows, nb, W), F32) for rows in (POOL_BUF, CONV_BUF)]
    operands = [x, spool, sconv, ck, cv, gpre, gpost, win, poolw, pscale, convw, convb, lng, lnb, wbr, wout]
    aliases = _carry_layers(operands, in_specs, carried, first_output=1)
    scratch = [
        pltpu.VMEM((n_slabs, (POOL_BUF + nt) * nseq, LANES), F32),
        pltpu.VMEM((n_slabs, (CONV_BUF + nt) * nseq, LANES), F32),
        pltpu.VMEM((N_XHEADS, T, XHEAD_DIM), F32),
        pltpu.VMEM((N_XHEADS, T, XHEAD_DIM), F32),
        pltpu.VMEM((D // LANES, T, LANES), F32),
        pltpu.VMEM((nseq, N_MEM * N_XHEADS, XHEAD_DIM), F32),
        pltpu.SemaphoreType.DMA(()),
    ]
    body = _skip_carried(_sample_kernel, len(operands) - len(aliases), len(aliases))
    return pl.pallas_call(
        functools.partial(body, nseq=nseq, layer=layer, first=first),
        grid=(nb // nseq,), in_specs=in_specs, out_specs=out_specs, out_shape=out_shape,
        scratch_shapes=scratch, name=f"sample_layer{layer}", input_output_aliases=aliases,
        compiler_params=pltpu.CompilerParams(
            dimension_semantics=("arbitrary",), vmem_limit_bytes=VMEM_LIMIT),
    )(*operands)


def kernel(x_prompt, x_sample, state_pool, state_conv, cache_mem_k, cache_mem_v, mem_prompt,
           norm_pre, norm_post, mem_norm, w_mem_kv, w_in, pool_w, pool_scale, conv_w, conv_b,
           conv_ln_g, conv_ln_b, w_branch, w_out):
    W = BRANCH_W
    nb, nt, D = x_sample.shape
    row = lambda p: p.reshape(DEPTH, 1, p.shape[-1])
    gpre, gpost, gmem = row(norm_pre), row(norm_post), row(mem_norm)
    pscale, convb, lng, lnb = row(pool_scale), row(conv_b), row(conv_ln_g), row(conv_ln_b)
    ck = cache_mem_k.reshape(DEPTH, nb, N_MEM * N_XHEADS, XHEAD_DIM)
    cv = cache_mem_v.reshape(DEPTH, nb, N_MEM * N_XHEADS, XHEAD_DIM)

    spool_t = state_pool.transpose(0, 2, 1, 3)
    sconv_t = state_conv.transpose(0, 2, 1, 3)
    xs = x_sample

    xp = x_prompt
    prompt_new, sample_new = None, None
    for i in range(DEPTH):
        xp, *prompt_new = _prompt_layer(
            xp, mem_prompt, i, prompt_new, gpre, gpost, gmem, w_mem_kv, w_in, pool_w, pscale, conv_w,
            convb, lng, lnb, w_branch, w_out)
        win, wbr, wout = prompt_new[-3:]
        prompt_new = prompt_new[:-3]
        xs, *sample_new = _sample_layer(
            xs, spool_t, sconv_t, ck, cv, i, sample_new, gpre, gpost, win, pool_w, pscale, conv_w,
            convb, lng, lnb, wbr, wout)
    pool_p, conv_p, mk_p, mv_p = prompt_new
    pool_s, conv_s = (st.transpose(0, 2, 1, 3) for st in sample_new)
    heads = lambda kv: kv.reshape(DEPTH, kv.shape[1], N_MEM, N_XHEADS, XHEAD_DIM)
    return (xp, xs, pool_p, conv_p, heads(mk_p), heads(mv_p), pool_s, conv_s)
```

```python
import functools
import math

import jax
import jax.numpy as jnp
from jax import lax
from jax.experimental import pallas as pl
from jax.experimental.pallas import tpu as pltpu

D_MODEL = 1024
DEPTH = 2
DEC_SEQ = 8
PAST_LEN = 16384
BRANCH_W = D_MODEL // 2
N_BRANCH = 3
POOL_WINDOWS = (2, 4, 8, 16)
POOL_GROUP_W = BRANCH_W // len(POOL_WINDOWS)
POOL_BUF = max(POOL_WINDOWS) - 1
CONV_WIDTH = 31
CONV_BUF = CONV_WIDTH - 1
N_MEM = 256
N_XHEADS = 4
XHEAD_DIM = BRANCH_W // N_XHEADS
N_IN_SLICES = 7
EPS = 1e-6

LANES = 128
SUBLANES = 8
POOL_PAD = 8
POOL_HEAD = POOL_PAD + 16
CONV_HEAD = 32
PROMPT_TILE = 256
SAMPLE_SEQS = 16
WIDE_STAGE_ROWS = 128
VMEM_LIMIT = 60 * 1024 * 1024
NEG_LOG2E = -1.0 / math.log(2.0)
MASKED_SCORE = -1e30

F32 = jnp.float32
BF16 = jnp.bfloat16


def _sigmoid(x):
    return 1.0 / (1.0 + jnp.exp2(x * NEG_LOG2E))


def _silu(x):
    return x * _sigmoid(x)


def _rmsnorm(x, g):
    return x * lax.rsqrt(jnp.mean(x * x, axis=-1, keepdims=True) + EPS) * g


def _dot(a, b):
    return jnp.dot(a, b, preferred_element_type=F32)


def _pool_finish(acc_parts, p_in, p_gate, cnt_parts, poolw_ref, pscale_ref):
    ys = []
    for g in range(len(POOL_WINDOWS)):
        c0 = g * POOL_GROUP_W
        mixed = acc_parts[g] / cnt_parts[g] - p_in[:, c0:c0 + POOL_GROUP_W]
        ys.append(_dot(mixed.astype(BF16), poolw_ref[g].astype(BF16)))
    y = jnp.concatenate(ys, axis=-1)
    return y * pscale_ref[...] * _silu(p_gate)


def _conv_finish(cv, c_gate, convb_ref, lng_ref, lnb_ref):
    cv = cv + convb_ref[...]
    mu = jnp.mean(cv, axis=-1, keepdims=True)
    d = cv - mu
    var = jnp.mean(d * d, axis=-1, keepdims=True)
    r = d * lax.rsqrt(var + EPS) * lng_ref[...] + lnb_ref[...]
    return _silu(r) * _silu(c_gate)


def _merge_out(x, h, branches, win_ref, wbr_ref, wout_ref, gpost_ref):
    merged = None
    base = N_IN_SLICES * BRANCH_W
    for n, br in enumerate(branches):
        proj = _dot(br.astype(BF16), wbr_ref[n])
        gate = _sigmoid(_dot(h, win_ref[:, base + n * D_MODEL:base + (n + 1) * D_MODEL]))
        term = gate * proj
        merged = term if merged is None else merged + term
    y = _dot(merged.astype(BF16), wout_ref[...])
    return x + _rmsnorm(y, gpost_ref[...])


def _own_layer(ref, layer, first):
    return ref.at[layer] if first else ref


def _zero_other_layers(ref, layer):
    for other in range(ref.shape[0]):
        if other != layer:
            ref[other] = jnp.zeros(ref.shape[1:], ref.dtype)


def _stage_and_cast(streams):
    copies = [[pltpu.make_async_copy(src, stage.at[i % 2], sem.at[i % 2])
               for i, (src, _) in enumerate(chunks)] for chunks, stage, sem in streams]
    for stream in copies:
        stream[0].start()
    for i in range(max(len(stream) for stream in copies)):
        for stream in copies:
            if i + 1 < len(stream):
                stream[i + 1].start()
        for stream, (chunks, stage, _) in zip(copies, streams):
            if i < len(stream):
                stream[i].wait()
                chunks[i][1][...] = stage[i % 2].astype(BF16)


def _prompt_kernel(x_ref, mem_ref, gpre_ref, gpost_ref, gmem_ref, wkv_hbm, win_hbm, poolw_ref,
                   pscale_ref, convw_ref, convb_ref, lng_ref, lnb_ref, wbr_hbm, wout_hbm,
                   y_ref, npool_ref, nconv_ref, mk_ref, mv_ref, win_out, wbr_out, wout_out,
                   pool_ext, pool_tmp, conv_ext, conv_sh, gate_s, kt_s, v_s,
                   wkv_ref, win_ref, wbr_ref, wout_ref, stage_wide, stage_narrow, sem_wide, sem_narrow,
                   sem_out, *, ts, layer, first):
    W = BRANCH_W
    s = pl.program_id(1)
    first_step = jnp.logical_and(pl.program_id(0) == 0, s == 0)
    per_layer = (npool_ref, nconv_ref, mk_ref, mv_ref)
    npool_ref, nconv_ref, mk_ref, mv_ref = (_own_layer(r, layer, first) for r in per_layer)

    def hand_over():
        return [pltpu.make_async_copy(src, dst, sem_out.at[i]) for i, (src, dst) in enumerate(
            ((win_ref, win_out), (wbr_ref, wbr_out), (wout_ref, wout_out)))]

    @pl.when(first_step)
    def _():
        nrows = stage_narrow.shape[1]
        narrow = [(wkv_hbm.at[layer, pl.ds(r, nrows), :], wkv_ref.at[pl.ds(r, nrows), :])
                  for r in range(0, D_MODEL, nrows)]
        narrow += [(wbr_hbm.at[layer, n], wbr_ref.at[n]) for n in range(N_BRANCH)]
        narrow += [(wout_hbm.at[layer, pl.ds(r, nrows), :], wout_ref.at[pl.ds(r, nrows), :])
                   for r in range(0, D_MODEL, nrows)]
        wrows = stage_wide.shape[1]
        wide = [(win_hbm.at[layer, pl.ds(r, wrows), :], win_ref.at[pl.ds(r, wrows), :])
                for r in range(0, D_MODEL, wrows)]
        _stage_and_cast([(narrow, stage_narrow, sem_narrow), (wide, stage_wide, sem_wide)])
        for copy in hand_over():
            copy.start()

    @pl.when(s == 0)
    def _():
        if first:
            for r in per_layer:
                _zero_other_layers(r, layer)
        pool_ext[0:POOL_HEAD, :] = jnp.zeros((POOL_HEAD, W), F32)
        pool_tmp[:, 0:POOL_PAD, :] = jnp.zeros((pool_tmp.shape[0], POOL_PAD, POOL_GROUP_W), F32)
        conv_ext[0:CONV_HEAD, :] = jnp.zeros((CONV_HEAD, W), F32)
        hm = _rmsnorm(mem_ref[0], gmem_ref[...]).astype(BF16)
        kv = _dot(hm, wkv_ref[...])
        k = kv[:, :W]
        v = kv[:, W:]
        for hd in range(N_XHEADS):
            rows = pl.ds(hd, N_MEM, stride=N_XHEADS)
            mk_ref[0, rows, :] = k[:, hd * XHEAD_DIM:(hd + 1) * XHEAD_DIM]
            mv_ref[0, rows, :] = v[:, hd * XHEAD_DIM:(hd + 1) * XHEAD_DIM]
        kt_s[...] = k.T.astype(BF16)
        v_s[...] = v.astype(BF16)

    x = x_ref[0]
    h = _rmsnorm(x, gpre_ref[...]).astype(BF16)

    pb = _dot(h, win_ref[:, 2 * W:5 * W])
    u = pb[:, :W] * _sigmoid(pb[:, W:2 * W])
    c_gate = pb[:, 2 * W:]
    conv_ext[CONV_HEAD:CONV_HEAD + ts, :] = u
    for r in range(1, SUBLANES):
        conv_sh[r - 1, :, :] = conv_ext[r:r + ts + CONV_HEAD - SUBLANES, :]

    off = CONV_HEAD - CONV_BUF
    gate_base = N_IN_SLICES * W
    n_slabs = W // LANES
    gate_chunk = N_BRANCH * D_MODEL // n_slabs
    slabs = []
    for c in range(n_slabs):
        g0 = c * gate_chunk
        gate_s[:, g0:g0 + gate_chunk] = _sigmoid(
            _dot(h, win_ref[:, gate_base + g0:gate_base + g0 + gate_chunk]))
        c0 = c * LANES
        acc = None
        for k in range(CONV_WIDTH):
            a8, r = divmod(off + k, SUBLANES)
            if r == 0:
                rows = conv_ext[a8 * SUBLANES:a8 * SUBLANES + ts, c0:c0 + LANES]
            else:
                rows = conv_sh[r - 1, a8 * SUBLANES:a8 * SUBLANES + ts, c0:c0 + LANES]
            term = rows * convw_ref[k:k + 1, c0:c0 + LANES]
            acc = term if acc is None else acc + term
        slabs.append(acc)
    bconv = _conv_finish(jnp.concatenate(slabs, axis=-1), c_gate, convb_ref, lng_ref, lnb_ref)
    nconv_ref[0] = conv_ext[CONV_HEAD + ts - CONV_BUF:CONV_HEAD + ts, :]
    conv_ext[0:CONV_HEAD, :] = conv_ext[ts:ts + CONV_HEAD, :]

    pa = _dot(h, win_ref[:, 0:2 * W])
    p_in = pa[:, :W]
    p_gate = pa[:, W:]
    pool_ext[POOL_HEAD:POOL_HEAD + ts, :] = p_in
    pos1 = (lax.broadcasted_iota(jnp.int32, (ts, POOL_GROUP_W), 0) + s * ts + 1).astype(F32)
    accs, cnts = [], []
    for g, win in enumerate(POOL_WINDOWS):
        c0 = g * POOL_GROUP_W
        level, shift = None, 1
        while shift < win:
            last = 2 * shift == win
            lo = POOL_HEAD if last else SUBLANES
            n = POOL_HEAD + ts - lo
            if level is None:
                cur = (pool_ext[lo:lo + n, c0:c0 + POOL_GROUP_W]
                       + pool_ext[lo - shift:lo - shift + n, c0:c0 + POOL_GROUP_W])
            else:
                cur = pool_tmp[level, lo:lo + n, :] + pool_tmp[level, lo - shift:lo - shift + n, :]
            if last:
                accs.append(cur)
            else:
                level = 0 if level is None else level + 1
                pool_tmp[level, lo:lo + n, :] = cur
            shift *= 2
        cnts.append(jnp.minimum(pos1, float(win)))
    a = _pool_finish(accs, p_in, p_gate, cnts, poolw_ref, pscale_ref)
    npool_ref[0] = pool_ext[POOL_HEAD + ts - POOL_BUF:POOL_HEAD + ts, :]
    pool_ext[POOL_PAD:POOL_HEAD, :] = pool_ext[ts + POOL_PAD:ts + POOL_HEAD, :]

    pc = _dot(h, win_ref[:, 5 * W:7 * W])
    q = pc[:, :W]
    x_gate = pc[:, W:]
    scale = 1.0 / math.sqrt(XHEAD_DIM)
    outs = []
    for hd in range(N_XHEADS):
        c0 = hd * XHEAD_DIM
        sc = _dot((q[:, c0:c0 + XHEAD_DIM] * scale).astype(BF16), kt_s[c0:c0 + XHEAD_DIM, :])
        e = jnp.exp(sc - jnp.max(sc, axis=-1, keepdims=True))
        l = jnp.sum(e, axis=-1, keepdims=True)
        outs.append(_dot(e.astype(BF16), v_s[:, c0:c0 + XHEAD_DIM]) / l)
    cattn = jnp.concatenate(outs, axis=-1) * _silu(x_gate)

    merged = None
    for n, br in enumerate((a, bconv, cattn)):
        term = gate_s[:, n * D_MODEL:(n + 1) * D_MODEL] * _dot(br.astype(BF16), wbr_ref[n])
        merged = term if merged is None else merged + term
    y = _dot(merged.astype(BF16), wout_ref[...])
    y_ref[0] = x + _rmsnorm(y, gpost_ref[...])

    @pl.when(first_step)
    def _():
        for copy in hand_over():
            copy.wait()


def _sample_kernel(x_ref, spool_ref, sconv_ref, k_ref, v_ref, gpre_ref, gpost_ref, win_ref,
                   poolw_ref, pscale_ref, convw_ref, convb_ref, lng_ref, lnb_ref, wbr_ref, wout_ref,
                   y_ref, npool_ref, nconv_ref, pool_ext, conv_ext, q_s, o_s, xy_s, v_buf, v_sem,
                   *, nseq, layer, first):
    W = BRANCH_W
    T = nseq * DEC_SEQ
    n_slabs = W // LANES
    step = pl.program_id(0)
    if first:
        _zero_other_layers(npool_ref, layer)
        _zero_other_layers(nconv_ref, layer)
    npool_ref = _own_layer(npool_ref, layer, first)
    nconv_ref = _own_layer(nconv_ref, layer, first)

    def v_copy(block):
        return pltpu.make_async_copy(v_ref.at[layer, pl.ds(block * nseq, nseq)], v_buf, v_sem)

    @pl.when(step == 0)
    def _():
        v_copy(0).start()

    x_seq = x_ref[...].reshape(T, D_MODEL)
    for c in range(D_MODEL // LANES):
        xy_s[c] = x_seq[:, c * LANES:(c + 1) * LANES]
    x = jnp.concatenate([
        jnp.concatenate([xy_s[c, pl.ds(t, nseq, stride=DEC_SEQ), :] for c in range(D_MODEL // LANES)], axis=-1)
        for t in range(DEC_SEQ)], axis=0)
    h = _rmsnorm(x, gpre_ref[...]).astype(BF16)

    pa = _dot(h, win_ref[:, 0:2 * W])
    p_in = pa[:, :W]
    p_gate = pa[:, W:]
    pos1 = (lax.broadcasted_iota(jnp.int32, (DEC_SEQ, nseq, POOL_GROUP_W), 0) + PAST_LEN + 1).astype(F32)
    pos1 = pos1.reshape(T, POOL_GROUP_W)
    accs, cnts = [], []
    for g, win in enumerate(POOL_WINDOWS):
        c0 = g * POOL_GROUP_W
        pool_ext[g, 0:POOL_BUF * nseq, :] = spool_ref[:, :, c0:c0 + POOL_GROUP_W].reshape(
            POOL_BUF * nseq, POOL_GROUP_W)
        pool_ext[g, POOL_BUF * nseq:(POOL_BUF + DEC_SEQ) * nseq, :] = p_in[:, c0:c0 + POOL_GROUP_W]
        acc = None
        for i in range(win):
            r0 = (POOL_BUF - i) * nseq
            rows = pool_ext[g, r0:r0 + T, :]
            acc = rows if acc is None else acc + rows
        accs.append(acc)
        cnts.append(jnp.minimum(pos1, float(win)))
    a = _pool_finish(accs, p_in, p_gate, cnts, poolw_ref, pscale_ref)
    for g in range(n_slabs):
        npool_ref[:, :, g * LANES:(g + 1) * LANES] = pool_ext[
            g, DEC_SEQ * nseq:(DEC_SEQ + POOL_BUF) * nseq, :].reshape(POOL_BUF, nseq, LANES)

    pb = _dot(h, win_ref[:, 2 * W:5 * W])
    u = pb[:, :W] * _sigmoid(pb[:, W:2 * W])
    c_gate = pb[:, 2 * W:]
    slabs = []
    for c in range(n_slabs):
        c0 = c * LANES
        conv_ext[c, 0:CONV_BUF * nseq, :] = sconv_ref[:, :, c0:c0 + LANES].reshape(CONV_BUF * nseq, LANES)
        conv_ext[c, CONV_BUF * nseq:(CONV_BUF + DEC_SEQ) * nseq, :] = u[:, c0:c0 + LANES]
        acc = None
        for k in range(CONV_WIDTH):
            term = conv_ext[c, k * nseq:k * nseq + T, :] * convw_ref[k:k + 1, c0:c0 + LANES]
            acc = term if acc is None else acc + term
        slabs.append(acc)
    bconv = _conv_finish(jnp.concatenate(slabs, axis=-1), c_gate, convb_ref, lng_ref, lnb_ref)
    for c in range(n_slabs):
        nconv_ref[:, :, c * LANES:(c + 1) * LANES] = conv_ext[
            c, DEC_SEQ * nseq:(DEC_SEQ + CONV_BUF) * nseq, :].reshape(CONV_BUF, nseq, LANES)

    pc = _dot(h, win_ref[:, 5 * W:7 * W])
    x_gate = pc[:, W:]
    for hd in range(N_XHEADS):
        q_s[hd] = pc[:, hd * XHEAD_DIM:(hd + 1) * XHEAD_DIM]
    scale = 1.0 / math.sqrt(XHEAD_DIM)
    nq = N_XHEADS * DEC_SEQ
    nk = N_MEM * N_XHEADS
    q4 = jnp.stack([
        jnp.concatenate([q_s[hd, pl.ds(sq, DEC_SEQ, stride=nseq), :] for hd in range(N_XHEADS)], axis=0)
        for sq in range(nseq)]).astype(BF16)
    sc = jnp.einsum('gqd,gkd->gqk', q4, k_ref[...].astype(BF16), preferred_element_type=F32) * scale
    q_head = lax.broadcasted_iota(jnp.int32, (nq, nk), 0) // DEC_SEQ
    k_head = lax.broadcasted_iota(jnp.int32, (nq, nk), 1) % N_XHEADS
    sc = jnp.where((q_head == k_head)[None], sc, MASKED_SCORE)
    e = jnp.exp(sc - jnp.max(sc, axis=-1, keepdims=True))
    l = jnp.sum(e, axis=-1, keepdims=True)
    v_copy(step).wait()
    o = jnp.einsum('gqk,gkd->gqd', e.astype(BF16), v_buf[...].astype(BF16),
                   preferred_element_type=F32) / l
    for sq in range(nseq):
        for hd in range(N_XHEADS):
            o_s[hd, pl.ds(sq, DEC_SEQ, stride=nseq), :] = o[sq, hd * DEC_SEQ:(hd + 1) * DEC_SEQ, :]
    cattn = jnp.concatenate([o_s[hd] for hd in range(N_XHEADS)], axis=-1) * _silu(x_gate)

    y = _merge_out(x, h, (a, bconv, cattn), win_ref, wbr_ref, wout_ref, gpost_ref)
    for c in range(D_MODEL // LANES):
        xy_s[c] = y[:, c * LANES:(c + 1) * LANES]
    for sq in range(nseq):
        y_ref[sq] = jnp.concatenate(
            [xy_s[c, pl.ds(sq, DEC_SEQ, stride=nseq), :] for c in range(D_MODEL // LANES)], axis=-1)

    @pl.when(step + 1 < pl.num_programs(0))
    def _():
        v_copy(step + 1).start()


def _layer_spec(shape, layer):
    zeros = (0,) * len(shape)
    return pl.BlockSpec((None,) + shape, lambda *_: (layer,) + zeros, pipeline_mode=pl.Buffered(1))


def _whole_spec(shape):
    zeros = (0,) * len(shape)
    return pl.BlockSpec(shape, lambda *_: zeros, pipeline_mode=pl.Buffered(1))


def _carry_layers(operands, in_specs, carried, first_output):
    aliases = {}
    for i, arr in enumerate(carried or ()):
        aliases[len(operands)] = first_output + i
        operands.append(arr)
        in_specs.append(pl.BlockSpec(memory_space=pl.ANY))
    return aliases


def _per_layer_spec(block, index_map, layer, first):
    if first:
        return pl.BlockSpec((DEPTH,) + block, lambda *i: (0,) + index_map(*i))
    return pl.BlockSpec((None,) + block, lambda *i: (layer,) + index_map(*i))


def _skip_carried(body, n_inputs, n_carried):
    def wrapped(*refs, **kw):
        return body(*refs[:n_inputs], *refs[n_inputs + n_carried:], **kw)
    return wrapped


def _prompt_layer(x, mem, layer, carried, gpre, gpost, gmem, wkv, win, poolw, pscale, convw, convb,
                  lng, lnb, wbr, wout):
    B, S, D = x.shape
    W = BRANCH_W
    ts = PROMPT_TILE
    grid = (B, S // ts)
    small = lambda width: _layer_spec((1, width), layer)
    in_hbm = pl.BlockSpec(memory_space=pl.ANY)
    in_specs = [
        pl.BlockSpec((1, ts, D), lambda b, s: (b, s, 0)),
        pl.BlockSpec((1, N_MEM, D), lambda b, s: (b, 0, 0)),
        small(D), small(D), small(D),
        in_hbm, in_hbm,
        _layer_spec(poolw.shape[1:], layer),
        small(W),
        _layer_spec((CONV_WIDTH, W), layer),
        small(W), small(W), small(W),
        in_hbm, in_hbm,
    ]
    per_layer = [(POOL_BUF, W), (CONV_BUF, W), (N_MEM * N_XHEADS, XHEAD_DIM), (N_MEM * N_XHEADS, XHEAD_DIM)]
    first = carried is None
    out_specs = [pl.BlockSpec((1, ts, D), lambda b, s: (b, s, 0))] + [
        _per_layer_spec((1,) + shape, lambda b, s: (b, 0, 0), layer, first) for shape in per_layer]
    out_shape = [jax.ShapeDtypeStruct((B, S, D), F32)] + [
        jax.ShapeDtypeStruct((DEPTH, B) + shape, F32) for shape in per_layer]
    handed = [win.shape[1:], wbr.shape[1:], wout.shape[1:]]
    out_specs += [pl.BlockSpec(memory_space=pl.ANY)] * len(handed)
    out_shape += [jax.ShapeDtypeStruct(shape, BF16) for shape in handed]
    operands = [x, mem, gpre, gpost, gmem, wkv, win, poolw, pscale, convw, convb, lng, lnb, wbr, wout]
    aliases = _carry_layers(operands, in_specs, carried, first_output=1)
    scratch = [
        pltpu.VMEM((POOL_HEAD + ts, W), F32),
        pltpu.VMEM((len(POOL_WINDOWS) - 1, POOL_HEAD + ts, POOL_GROUP_W), F32),
        pltpu.VMEM((CONV_HEAD + ts, W), F32),
        pltpu.VMEM((SUBLANES - 1, ts + CONV_HEAD - SUBLANES, W), F32),
        pltpu.VMEM((ts, N_BRANCH * D_MODEL), F32),
        pltpu.VMEM((W, N_MEM), BF16),
        pltpu.VMEM((N_MEM, W), BF16),
        pltpu.VMEM(wkv.shape[1:], BF16),
        pltpu.VMEM(win.shape[1:], BF16),
        pltpu.VMEM(wbr.shape[1:], BF16),
        pltpu.VMEM(wout.shape[1:], BF16),
        pltpu.VMEM((2, WIDE_STAGE_ROWS, win.shape[-1]), F32),
        pltpu.VMEM((2, W, D), F32),
        pltpu.SemaphoreType.DMA((2,)),
        pltpu.SemaphoreType.DMA((2,)),
        pltpu.SemaphoreType.DMA((len(handed),)),
    ]
    body = _skip_carried(_prompt_kernel, len(operands) - len(aliases), len(aliases))
    return pl.pallas_call(
        functools.partial(body, ts=ts, layer=layer, first=first),
        grid=grid, in_specs=in_specs, out_specs=out_specs, out_shape=out_shape,
        scratch_shapes=scratch, name=f"prompt_layer{layer}", input_output_aliases=aliases,
        compiler_params=pltpu.CompilerParams(
            dimension_semantics=("arbitrary", "arbitrary"), vmem_limit_bytes=VMEM_LIMIT),
    )(*operands)


def _sample_layer(x, spool, sconv, ck, cv, layer, carried, gpre, gpost, win, poolw, pscale, convw, convb,
                  lng, lnb, wbr, wout):
    nb, nt, D = x.shape
    W = BRANCH_W
    nseq = SAMPLE_SEQS
    T = nseq * nt
    n_slabs = W // LANES
    small = lambda width: _layer_spec((1, width), layer)
    state_in = lambda rows: pl.BlockSpec((None, rows, nseq, W), lambda g: (layer, 0, g, 0))
    cache = pl.BlockSpec((None, nseq, N_MEM * N_XHEADS, XHEAD_DIM), lambda g: (layer, g, 0, 0))
    in_specs = [
        pl.BlockSpec((nseq, nt, D), lambda g: (g, 0, 0)),
        state_in(POOL_BUF), state_in(CONV_BUF), cache, pl.BlockSpec(memory_space=pl.ANY),
        small(D), small(D),
        _whole_spec(win.shape),
        _layer_spec(poolw.shape[1:], layer),
        small(W),
        _layer_spec((CONV_WIDTH, W), layer),
        small(W), small(W), small(W),
        _whole_spec(wbr.shape),
        _whole_spec(wout.shape),
    ]
    first = carried is None
    out_specs = [pl.BlockSpec((nseq, nt, D), lambda g: (g, 0, 0))] + [
        _per_layer_spec((rows, nseq, W), lambda g: (0, g, 0), layer, first) for rows in (POOL_BUF, CONV_BUF)]
    out_shape = [jax.ShapeDtypeStruct((nb, nt, D), F32)] + [
        jax.ShapeDtypeStruct((DEPTH, rows, nb, W), F32) for rows in (POOL_BUF, CONV_BUF)]
    operands = [x, spool, sconv, ck, cv, gpre, gpost, win, poolw, pscale, convw, convb, lng, lnb, wbr, wout]
    aliases = _carry_layers(operands, in_specs, carried, first_output=1)
    scratch = [
        pltpu.VMEM((n_slabs, (POOL_BUF + nt) * nseq, LANES), F32),
        pltpu.VMEM((n_slabs, (CONV_BUF + nt) * nseq, LANES), F32),
        pltpu.VMEM((N_XHEADS, T, XHEAD_DIM), F32),
        pltpu.VMEM((N_XHEADS, T, XHEAD_DIM), F32),
        pltpu.VMEM((D // LANES, T, LANES), F32),
        pltpu.VMEM((nseq, N_MEM * N_XHEADS, XHEAD_DIM), F32),
        pltpu.SemaphoreType.DMA(()),
    ]
    body = _skip_carried(_sample_kernel, len(operands) - len(aliases), len(aliases))
    return pl.pallas_call(
        functools.partial(body, nseq=nseq, layer=layer, first=first),
        grid=(nb // nseq,), in_specs=in_specs, out_specs=out_specs, out_shape=out_shape,
        scratch_shapes=scratch, name=f"sample_layer{layer}", input_output_aliases=aliases,
        compiler_params=pltpu.CompilerParams(
            dimension_semantics=("arbitrary",), vmem_limit_bytes=VMEM_LIMIT),
    )(*operands)


def kernel(x_prompt, x_sample, state_pool, state_conv, cache_mem_k, cache_mem_v, mem_prompt,
           norm_pre, norm_post, mem_norm, w_mem_kv, w_in, pool_w, pool_scale, conv_w, conv_b,
           conv_ln_g, conv_ln_b, w_branch, w_out):
    W = BRANCH_W
    nb, nt, D = x_sample.shape
    row = lambda p: p.reshape(DEPTH, 1, p.shape[-1])
    gpre, gpost, gmem = row(norm_pre), row(norm_post), row(mem_norm)
    pscale, convb, lng, lnb = row(pool_scale), row(conv_b), row(conv_ln_g), row(conv_ln_b)
    ck = cache_mem_k.reshape(DEPTH, nb, N_MEM * N_XHEADS, XHEAD_DIM)
    cv = cache_mem_v.reshape(DEPTH, nb, N_MEM * N_XHEADS, XHEAD_DIM)

    spool_t = state_pool.transpose(0, 2, 1, 3)
    sconv_t = state_conv.transpose(0, 2, 1, 3)
    xs = x_sample

    xp = x_prompt
    prompt_new, sample_new = None, None
    for i in range(DEPTH):
        xp, *prompt_new = _prompt_layer(
            xp, mem_prompt, i, prompt_new, gpre, gpost, gmem, w_mem_kv, w_in, pool_w, pscale, conv_w,
            convb, lng, lnb, w_branch, w_out)
        win, wbr, wout = prompt_new[-3:]
        prompt_new = prompt_new[:-3]
        xs, *sample_new = _sample_layer(
            xs, spool_t, sconv_t, ck, cv, i, sample_new, gpre, gpost, win, pool_w, pscale, conv_w,
            convb, lng, lnb, wbr, wout)
    pool_p, conv_p, mk_p, mv_p = prompt_new
    pool_s, conv_s = (st.transpose(0, 2, 1, 3) for st in sample_new)
    heads = lambda kv: kv.reshape(DEPTH, kv.shape[1], N_MEM, N_XHEADS, XHEAD_DIM)
    return (xp, xs, pool_p, conv_p, heads(mk_p), heads(mv_p), pool_s, conv_s)
```

```python
import functools
import math

import jax
import jax.numpy as jnp
from jax import lax
from jax.experimental import pallas as pl
from jax.experimental.pallas import tpu as pltpu

D_MODEL = 1024
DEPTH = 2
DEC_SEQ = 8
PAST_LEN = 16384
BRANCH_W = D_MODEL // 2
N_BRANCH = 3
POOL_WINDOWS = (2, 4, 8, 16)
POOL_GROUP_W = BRANCH_W // len(POOL_WINDOWS)
POOL_BUF = max(POOL_WINDOWS) - 1
CONV_WIDTH = 31
CONV_BUF = CONV_WIDTH - 1
N_MEM = 256
N_XHEADS = 4
XHEAD_DIM = BRANCH_W // N_XHEADS
N_IN_SLICES = 7
EPS = 1e-6

LANES = 128
SUBLANES = 8
POOL_PAD = 8
POOL_HEAD = POOL_PAD + 16
CONV_HEAD = 32
PROMPT_TILE = 256
SAMPLE_SEQS = 16
WIDE_STAGE_ROWS = 128
VMEM_LIMIT = 60 * 1024 * 1024
NEG_LOG2E = -1.0 / math.log(2.0)
MASKED_SCORE = float("-inf")

F32 = jnp.float32
BF16 = jnp.bfloat16


def _sigmoid(x):
    return 1.0 / (1.0 + jnp.exp2(x * NEG_LOG2E))


def _silu(x):
    return x * _sigmoid(x)


def _rmsnorm(x, g):
    return x * lax.rsqrt(jnp.mean(x * x, axis=-1, keepdims=True) + EPS) * g


def _dot(a, b):
    return jnp.dot(a, b, preferred_element_type=F32)


def _pool_finish(acc_parts, p_in, p_gate, cnt_parts, poolw_ref, pscale_ref):
    ys = []
    for g in range(len(POOL_WINDOWS)):
        c0 = g * POOL_GROUP_W
        mixed = acc_parts[g] / cnt_parts[g] - p_in[:, c0:c0 + POOL_GROUP_W]
        ys.append(_dot(mixed.astype(BF16), poolw_ref[g].astype(BF16)))
    y = jnp.concatenate(ys, axis=-1)
    return y * pscale_ref[...] * _silu(p_gate)


def _conv_finish(cv, c_gate, convb_ref, lng_ref, lnb_ref):
    cv = cv + convb_ref[...]
    mu = jnp.mean(cv, axis=-1, keepdims=True)
    d = cv - mu
    var = jnp.mean(d * d, axis=-1, keepdims=True)
    r = d * lax.rsqrt(var + EPS) * lng_ref[...] + lnb_ref[...]
    return _silu(r) * _silu(c_gate)


def _merge_out(x, h, branches, win_ref, wbr_ref, wout_ref, gpost_ref):
    merged = None
    base = N_IN_SLICES * BRANCH_W
    for n, br in enumerate(branches):
        proj = _dot(br.astype(BF16), wbr_ref[n])
        gate = _sigmoid(_dot(h, win_ref[:, base + n * D_MODEL:base + (n + 1) * D_MODEL]))
        term = gate * proj
        merged = term if merged is None else merged + term
    y = _dot(merged.astype(BF16), wout_ref[...])
    return x + _rmsnorm(y, gpost_ref[...])


def _own_layer(ref, layer, first):
    return ref.at[layer] if first else ref


def _zero_other_layers(ref, layer):
    for other in range(ref.shape[0]):
        if other != layer:
            ref[other] = jnp.zeros(ref.shape[1:], ref.dtype)


def _stage_and_cast(streams):
    copies = [[pltpu.make_async_copy(src, stage.at[i % 2], sem.at[i % 2])
               for i, (src, _) in enumerate(chunks)] for chunks, stage, sem in streams]
    for stream in copies:
        stream[0].start()
    for i in range(max(len(stream) for stream in copies)):
        for stream in copies:
            if i + 1 < len(stream):
                stream[i + 1].start()
        for stream, (chunks, stage, _) in zip(copies, streams):
            if i < len(stream):
                stream[i].wait()
                chunks[i][1][...] = stage[i % 2].astype(BF16)


def _prompt_kernel(x_ref, mem_ref, gpre_ref, gpost_ref, gmem_ref, wkv_hbm, win_hbm, poolw_ref,
                   pscale_ref, convw_ref, convb_ref, lng_ref, lnb_ref, wbr_hbm, wout_hbm,
                   y_ref, npool_ref, nconv_ref, mk_ref, mv_ref, win_out, wbr_out, wout_out,
                   pool_ext, pool_tmp, conv_ext, conv_sh, gate_s, kt_s, v_s,
                   wkv_ref, win_ref, wbr_ref, wout_ref, stage_wide, stage_narrow, sem_wide, sem_narrow,
                   sem_out, *, ts, layer, first):
    W = BRANCH_W
    s = pl.program_id(1)
    first_step = jnp.logical_and(pl.program_id(0) == 0, s == 0)
    per_layer = (npool_ref, nconv_ref, mk_ref, mv_ref)
    npool_ref, nconv_ref, mk_ref, mv_ref = (_own_layer(r, layer, first) for r in per_layer)

    def hand_over():
        return [pltpu.make_async_copy(src, dst, sem_out.at[i]) for i, (src, dst) in enumerate(
            ((win_ref, win_out), (wbr_ref, wbr_out), (wout_ref, wout_out)))]

    @pl.when(first_step)
    def _():
        nrows = stage_narrow.shape[1]
        narrow = [(wkv_hbm.at[layer, pl.ds(r, nrows), :], wkv_ref.at[pl.ds(r, nrows), :])
                  for r in range(0, D_MODEL, nrows)]
        narrow += [(wbr_hbm.at[layer, n], wbr_ref.at[n]) for n in range(N_BRANCH)]
        narrow += [(wout_hbm.at[layer, pl.ds(r, nrows), :], wout_ref.at[pl.ds(r, nrows), :])
                   for r in range(0, D_MODEL, nrows)]
        wrows = stage_wide.shape[1]
        wide = [(win_hbm.at[layer, pl.ds(r, wrows), :], win_ref.at[pl.ds(r, wrows), :])
                for r in range(0, D_MODEL, wrows)]
        _stage_and_cast([(narrow, stage_narrow, sem_narrow), (wide, stage_wide, sem_wide)])
        for copy in hand_over():
            copy.start()

    @pl.when(s == 0)
    def _():
        if first:
            for r in per_layer:
                _zero_other_layers(r, layer)
        pool_ext[0:POOL_HEAD, :] = jnp.zeros((POOL_HEAD, W), F32)
        pool_tmp[:, 0:POOL_PAD, :] = jnp.zeros((pool_tmp.shape[0], POOL_PAD, POOL_GROUP_W), F32)
        conv_ext[0:CONV_HEAD, :] = jnp.zeros((CONV_HEAD, W), F32)
        hm = _rmsnorm(mem_ref[0], gmem_ref[...]).astype(BF16)
        kv = _dot(hm, wkv_ref[...])
        k = kv[:, :W]
        v = kv[:, W:]
        for hd in range(N_XHEADS):
            rows = pl.ds(hd, N_MEM, stride=N_XHEADS)
            mk_ref[0, rows, :] = k[:, hd * XHEAD_DIM:(hd + 1) * XHEAD_DIM]
            mv_ref[0, rows, :] = v[:, hd * XHEAD_DIM:(hd + 1) * XHEAD_DIM]
        kt_s[...] = k.T.astype(BF16)
        v_s[...] = v.astype(BF16)

    x = x_ref[0]
    h = _rmsnorm(x, gpre_ref[...]).astype(BF16)

    pb = _dot(h, win_ref[:, 2 * W:5 * W])
    u = pb[:, :W] * _sigmoid(pb[:, W:2 * W])
    c_gate = pb[:, 2 * W:]
    conv_ext[CONV_HEAD:CONV_HEAD + ts, :] = u
    for r in range(1, SUBLANES):
        conv_sh[r - 1, :, :] = conv_ext[r:r + ts + CONV_HEAD - SUBLANES, :]

    off = CONV_HEAD - CONV_BUF
    gate_base = N_IN_SLICES * W
    n_slabs = W // LANES
    gate_chunk = N_BRANCH * D_MODEL // n_slabs
    slabs = []
    for c in range(n_slabs):
        g0 = c * gate_chunk
        gate_s[:, g0:g0 + gate_chunk] = _sigmoid(
            _dot(h, win_ref[:, gate_base + g0:gate_base + g0 + gate_chunk]))
        c0 = c * LANES
        acc = None
        for k in range(CONV_WIDTH):
            a8, r = divmod(off + k, SUBLANES)
            if r == 0:
                rows = conv_ext[a8 * SUBLANES:a8 * SUBLANES + ts, c0:c0 + LANES]
            else:
                rows = conv_sh[r - 1, a8 * SUBLANES:a8 * SUBLANES + ts, c0:c0 + LANES]
            term = rows * convw_ref[k:k + 1, c0:c0 + LANES]
            acc = term if acc is None else acc + term
        slabs.append(acc)
    bconv = _conv_finish(jnp.concatenate(slabs, axis=-1), c_gate, convb_ref, lng_ref, lnb_ref)
    nconv_ref[0] = conv_ext[CONV_HEAD + ts - CONV_BUF:CONV_HEAD + ts, :]
    conv_ext[0:CONV_HEAD, :] = conv_ext[ts:ts + CONV_HEAD, :]

    pa = _dot(h, win_ref[:, 0:2 * W])
    p_in = pa[:, :W]
    p_gate = pa[:, W:]
    pool_ext[POOL_HEAD:POOL_HEAD + ts, :] = p_in
    pos1 = (lax.broadcasted_iota(jnp.int32, (ts, POOL_GROUP_W), 0) + s * ts + 1).astype(F32)
    accs, cnts = [], []
    for g, win in enumerate(POOL_WINDOWS):
        c0 = g * POOL_GROUP_W
        level, shift = None, 1
        while shift < win:
            last = 2 * shift == win
            lo = POOL_HEAD if last else SUBLANES
            n = POOL_HEAD + ts - lo
            if level is None:
                cur = (pool_ext[lo:lo + n, c0:c0 + POOL_GROUP_W]
                       + pool_ext[lo - shift:lo - shift + n, c0:c0 + POOL_GROUP_W])
            else:
                cur = pool_tmp[level, lo:lo + n, :] + pool_tmp[level, lo - shift:lo - shift + n, :]
            if last:
                accs.append(cur)
            else:
                level = 0 if level is None else level + 1
                pool_tmp[level, lo:lo + n, :] = cur
            shift *= 2
        cnts.append(jnp.minimum(pos1, float(win)))
    a = _pool_finish(accs, p_in, p_gate, cnts, poolw_ref, pscale_ref)
    npool_ref[0] = pool_ext[POOL_HEAD + ts - POOL_BUF:POOL_HEAD + ts, :]
    pool_ext[POOL_PAD:POOL_HEAD, :] = pool_ext[ts + POOL_PAD:ts + POOL_HEAD, :]

    pc = _dot(h, win_ref[:, 5 * W:7 * W])
    q = pc[:, :W]
    x_gate = pc[:, W:]
    scale = 1.0 / math.sqrt(XHEAD_DIM)
    outs = []
    for hd in range(N_XHEADS):
        c0 = hd * XHEAD_DIM
        sc = _dot((q[:, c0:c0 + XHEAD_DIM] * scale).astype(BF16), kt_s[c0:c0 + XHEAD_DIM, :])
        e = jnp.exp(sc - jnp.max(sc, axis=-1, keepdims=True))
        l = jnp.sum(e, axis=-1, keepdims=True)
        outs.append(_dot(e.astype(BF16), v_s[:, c0:c0 + XHEAD_DIM]) / l)
    cattn = jnp.concatenate(outs, axis=-1) * _silu(x_gate)

    merged = None
    for n, br in enumerate((a, bconv, cattn)):
        term = gate_s[:, n * D_MODEL:(n + 1) * D_MODEL] * _dot(br.astype(BF16), wbr_ref[n])
        merged = term if merged is None else merged + term
    y = _dot(merged.astype(BF16), wout_ref[...])
    y_ref[0] = x + _rmsnorm(y, gpost_ref[...])

    @pl.when(first_step)
    def _():
        for copy in hand_over():
            copy.wait()


def _sample_kernel(x_ref, spool_ref, sconv_ref, k_ref, v_ref, gpre_ref, gpost_ref, win_ref,
                   poolw_ref, pscale_ref, convw_ref, convb_ref, lng_ref, lnb_ref, wbr_ref, wout_ref,
                   y_ref, npool_ref, nconv_ref, pool_ext, conv_ext, q_s, o_s, xy_s, v_buf, v_sem,
                   *, nseq, layer, first):
    W = BRANCH_W
    T = nseq * DEC_SEQ
    n_slabs = W // LANES
    step = pl.program_id(0)
    if first:
        _zero_other_layers(npool_ref, layer)
        _zero_other_layers(nconv_ref, layer)
    npool_ref = _own_layer(npool_ref, layer, first)
    nconv_ref = _own_layer(nconv_ref, layer, first)

    def v_copy(block):
        return pltpu.make_async_copy(v_ref.at[layer, pl.ds(block * nseq, nseq)], v_buf, v_sem)

    @pl.when(step == 0)
    def _():
        v_copy(0).start()

    x_seq = x_ref[...].reshape(T, D_MODEL)
    for c in range(D_MODEL // LANES):
        xy_s[c] = x_seq[:, c * LANES:(c + 1) * LANES]
    x = jnp.concatenate([
        jnp.concatenate([xy_s[c, pl.ds(t, nseq, stride=DEC_SEQ), :] for c in range(D_MODEL // LANES)], axis=-1)
        for t in range(DEC_SEQ)], axis=0)
    h = _rmsnorm(x, gpre_ref[...]).astype(BF16)

    pa = _dot(h, win_ref[:, 0:2 * W])
    p_in = pa[:, :W]
    p_gate = pa[:, W:]
    pos1 = (lax.broadcasted_iota(jnp.int32, (DEC_SEQ, nseq, POOL_GROUP_W), 0) + PAST_LEN + 1).astype(F32)
    pos1 = pos1.reshape(T, POOL_GROUP_W)
    accs, cnts = [], []
    for g, win in enumerate(POOL_WINDOWS):
        c0 = g * POOL_GROUP_W
        pool_ext[g, 0:POOL_BUF * nseq, :] = spool_ref[:, :, c0:c0 + POOL_GROUP_W].reshape(
            POOL_BUF * nseq, POOL_GROUP_W)
        pool_ext[g, POOL_BUF * nseq:(POOL_BUF + DEC_SEQ) * nseq, :] = p_in[:, c0:c0 + POOL_GROUP_W]
        acc = None
        for i in range(win):
            r0 = (POOL_BUF - i) * nseq
            rows = pool_ext[g, r0:r0 + T, :]
            acc = rows if acc is None else acc + rows
        accs.append(acc)
        cnts.append(jnp.minimum(pos1, float(win)))
    a = _pool_finish(accs, p_in, p_gate, cnts, poolw_ref, pscale_ref)
    for g in range(n_slabs):
        npool_ref[:, :, g * LANES:(g + 1) * LANES] = pool_ext[
            g, DEC_SEQ * nseq:(DEC_SEQ + POOL_BUF) * nseq, :].reshape(POOL_BUF, nseq, LANES)

    pb = _dot(h, win_ref[:, 2 * W:5 * W])
    u = pb[:, :W] * _sigmoid(pb[:, W:2 * W])
    c_gate = pb[:, 2 * W:]
    slabs = []
    for c in range(n_slabs):
        c0 = c * LANES
        conv_ext[c, 0:CONV_BUF * nseq, :] = sconv_ref[:, :, c0:c0 + LANES].reshape(CONV_BUF * nseq, LANES)
        conv_ext[c, CONV_BUF * nseq:(CONV_BUF + DEC_SEQ) * nseq, :] = u[:, c0:c0 + LANES]
        acc = None
        for k in range(CONV_WIDTH):
            term = conv_ext[c, k * nseq:k * nseq + T, :] * convw_ref[k:k + 1, c0:c0 + LANES]
            acc = term if acc is None else acc + term
        slabs.append(acc)
    bconv = _conv_finish(jnp.concatenate(slabs, axis=-1), c_gate, convb_ref, lng_ref, lnb_ref)
    for c in range(n_slabs):
        nconv_ref[:, :, c * LANES:(c + 1) * LANES] = conv_ext[
            c, DEC_SEQ * nseq:(DEC_SEQ + CONV_BUF) * nseq, :].reshape(CONV_BUF, nseq, LANES)

    pc = _dot(h, win_ref[:, 5 * W:7 * W])
    x_gate = pc[:, W:]
    for hd in range(N_XHEADS):
        q_s[hd] = pc[:, hd * XHEAD_DIM:(hd + 1) * XHEAD_DIM]
    scale = 1.0 / math.sqrt(XHEAD_DIM)
    nq = N_XHEADS * DEC_SEQ
    nk = N_MEM * N_XHEADS
    q4 = jnp.stack([
        jnp.concatenate([q_s[hd, pl.ds(sq, DEC_SEQ, stride=nseq), :] for hd in range(N_XHEADS)], axis=0)
        for sq in range(nseq)]).astype(BF16)
    sc = jnp.einsum('gqd,gkd->gqk', q4, k_ref[...].astype(BF16), preferred_element_type=F32) * scale
    q_head = lax.broadcasted_iota(jnp.int32, (nq, nk), 0) // DEC_SEQ
    k_head = lax.broadcasted_iota(jnp.int32, (nq, nk), 1) % N_XHEADS
    sc = jnp.where((q_head == k_head)[None], sc, MASKED_SCORE)
    e = jnp.exp(sc - jnp.max(sc, axis=-1, keepdims=True))
    l = jnp.sum(e, axis=-1, keepdims=True)
    v_copy(step).wait()
    o = jnp.einsum('gqk,gkd->gqd', e.astype(BF16), v_buf[...].astype(BF16),
                   preferred_element_type=F32) / l
    for sq in range(nseq):
        for hd in range(N_XHEADS):
            o_s[hd, pl.ds(sq, DEC_SEQ, stride=nseq), :] = o[sq, hd * DEC_SEQ:(hd + 1) * DEC_SEQ, :]
    cattn = jnp.concatenate([o_s[hd] for hd in range(N_XHEADS)], axis=-1) * _silu(x_gate)

    y = _merge_out(x, h, (a, bconv, cattn), win_ref, wbr_ref, wout_ref, gpost_ref)
    for c in range(D_MODEL // LANES):
        xy_s[c] = y[:, c * LANES:(c + 1) * LANES]
    for sq in range(nseq):
        y_ref[sq] = jnp.concatenate(
            [xy_s[c, pl.ds(sq, DEC_SEQ, stride=nseq), :] for c in range(D_MODEL // LANES)], axis=-1)

    @pl.when(step + 1 < pl.num_programs(0))
    def _():
        v_copy(step + 1).start()


def _layer_spec(shape, layer):
    zeros = (0,) * len(shape)
    return pl.BlockSpec((None,) + shape, lambda *_: (layer,) + zeros, pipeline_mode=pl.Buffered(1))


def _whole_spec(shape):
    zeros = (0,) * len(shape)
    return pl.BlockSpec(shape, lambda *_: zeros, pipeline_mode=pl.Buffered(1))


def _carry_layers(operands, in_specs, carried, first_output):
    aliases = {}
    for i, arr in enumerate(carried or ()):
        aliases[len(operands)] = first_output + i
        operands.append(arr)
        in_specs.append(pl.BlockSpec(memory_space=pl.ANY))
    return aliases


def _per_layer_spec(block, index_map, layer, first):
    if first:
        return pl.BlockSpec((DEPTH,) + block, lambda *i: (0,) + index_map(*i))
    return pl.BlockSpec((None,) + block, lambda *i: (layer,) + index_map(*i))


def _skip_carried(body, n_inputs, n_carried):
    def wrapped(*refs, **kw):
        return body(*refs[:n_inputs], *refs[n_inputs + n_carried:], **kw)
    return wrapped


def _prompt_layer(x, mem, layer, carried, gpre, gpost, gmem, wkv, win, poolw, pscale, convw, convb,
                  lng, lnb, wbr, wout):
    B, S, D = x.shape
    W = BRANCH_W
    ts = PROMPT_TILE
    grid = (B, S // ts)
    small = lambda width: _layer_spec((1, width), layer)
    in_hbm = pl.BlockSpec(memory_space=pl.ANY)
    in_specs = [
        pl.BlockSpec((1, ts, D), lambda b, s: (b, s, 0)),
        pl.BlockSpec((1, N_MEM, D), lambda b, s: (b, 0, 0)),
        small(D), small(D), small(D),
        in_hbm, in_hbm,
        _layer_spec(poolw.shape[1:], layer),
        small(W),
        _layer_spec((CONV_WIDTH, W), layer),
        small(W), small(W), small(W),
        in_hbm, in_hbm,
    ]
    per_layer = [(POOL_BUF, W), (CONV_BUF, W), (N_MEM * N_XHEADS, XHEAD_DIM), (N_MEM * N_XHEADS, XHEAD_DIM)]
    first = carried is None
    out_specs = [pl.BlockSpec((1, ts, D), lambda b, s: (b, s, 0))] + [
        _per_layer_spec((1,) + shape, lambda b, s: (b, 0, 0), layer, first) for shape in per_layer]
    out_shape = [jax.ShapeDtypeStruct((B, S, D), F32)] + [
        jax.ShapeDtypeStruct((DEPTH, B) + shape, F32) for shape in per_layer]
    handed = [win.shape[1:], wbr.shape[1:], wout.shape[1:]]
    out_specs += [pl.BlockSpec(memory_space=pl.ANY)] * len(handed)
    out_shape += [jax.ShapeDtypeStruct(shape, BF16) for shape in handed]
    operands = [x, mem, gpre, gpost, gmem, wkv, win, poolw, pscale, convw, convb, lng, lnb, wbr, wout]
    aliases = _carry_layers(operands, in_specs, carried, first_output=1)
    scratch = [
        pltpu.VMEM((POOL_HEAD + ts, W), F32),
        pltpu.VMEM((len(POOL_WINDOWS) - 1, POOL_HEAD + ts, POOL_GROUP_W), F32),
        pltpu.VMEM((CONV_HEAD + ts, W), F32),
        pltpu.VMEM((SUBLANES - 1, ts + CONV_HEAD - SUBLANES, W), F32),
        pltpu.VMEM((ts, N_BRANCH * D_MODEL), F32),
        pltpu.VMEM((W, N_MEM), BF16),
        pltpu.VMEM((N_MEM, W), BF16),
        pltpu.VMEM(wkv.shape[1:], BF16),
        pltpu.VMEM(win.shape[1:], BF16),
        pltpu.VMEM(wbr.shape[1:], BF16),
        pltpu.VMEM(wout.shape[1:], BF16),
        pltpu.VMEM((2, WIDE_STAGE_ROWS, win.shape[-1]), F32),
        pltpu.VMEM((2, W, D), F32),
        pltpu.SemaphoreType.DMA((2,)),
        pltpu.SemaphoreType.DMA((2,)),
        pltpu.SemaphoreType.DMA((len(handed),)),
    ]
    body = _skip_carried(_prompt_kernel, len(operands) - len(aliases), len(aliases))
    return pl.pallas_call(
        functools.partial(body, ts=ts, layer=layer, first=first),
        grid=grid, in_specs=in_specs, out_specs=out_specs, out_shape=out_shape,
        scratch_shapes=scratch, name=f"prompt_layer{layer}", input_output_aliases=aliases,
        compiler_params=pltpu.CompilerParams(
            dimension_semantics=("arbitrary", "arbitrary"), vmem_limit_bytes=VMEM_LIMIT),
    )(*operands)


def _sample_layer(x, spool, sconv, ck, cv, layer, carried, gpre, gpost, win, poolw, pscale, convw, convb,
                  lng, lnb, wbr, wout):
    nb, nt, D = x.shape
    W = BRANCH_W
    nseq = SAMPLE_SEQS
    T = nseq * nt
    n_slabs = W // LANES
    small = lambda width: _layer_spec((1, width), layer)
    state_in = lambda rows: pl.BlockSpec((None, rows, nseq, W), lambda g: (layer, 0, g, 0))
    cache = pl.BlockSpec((None, nseq, N_MEM * N_XHEADS, XHEAD_DIM), lambda g: (layer, g, 0, 0))
    in_specs = [
        pl.BlockSpec((nseq, nt, D), lambda g: (g, 0, 0)),
        state_in(POOL_BUF), state_in(CONV_BUF), cache, pl.BlockSpec(memory_space=pl.ANY),
        small(D), small(D),
        _whole_spec(win.shape),
        _layer_spec(poolw.shape[1:], layer),
        small(W),
        _layer_spec((CONV_WIDTH, W), layer),
        small(W), small(W), small(W),
        _whole_spec(wbr.shape),
        _whole_spec(wout.shape),
    ]
    first = carried is None
    out_specs = [pl.BlockSpec((nseq, nt, D), lambda g: (g, 0, 0))] + [
        _per_layer_spec((rows, nseq, W), lambda g: (0, g, 0), layer, first) for rows in (POOL_BUF, CONV_BUF)]
    out_shape = [jax.ShapeDtypeStruct((nb, nt, D), F32)] + [
        jax.ShapeDtypeStruct((DEPTH, rows, nb, W), F32) for rows in (POOL_BUF, CONV_BUF)]
    operands = [x, spool, sconv, ck, cv, gpre, gpost, win, poolw, pscale, convw, convb, lng, lnb, wbr, wout]
    aliases = _carry_layers(operands, in_specs, carried, first_output=1)
    scratch = [
        pltpu.VMEM((n_slabs, (POOL_BUF + nt) * nseq, LANES), F32),
        pltpu.VMEM((n_slabs, (CONV_BUF + nt) * nseq, LANES), F32),
        pltpu.VMEM((N_XHEADS, T, XHEAD_DIM), F32),
        pltpu.VMEM((N_XHEADS, T, XHEAD_DIM), F32),
        pltpu.VMEM((D // LANES, T, LANES), F32),
        pltpu.VMEM((nseq, N_MEM * N_XHEADS, XHEAD_DIM), F32),
        pltpu.SemaphoreType.DMA(()),
    ]
    body = _skip_carried(_sample_kernel, len(operands) - len(aliases), len(aliases))
    return pl.pallas_call(
        functools.partial(body, nseq=nseq, layer=layer, first=first),
        grid=(nb // nseq,), in_specs=in_specs, out_specs=out_specs, out_shape=out_shape,
        scratch_shapes=scratch, name=f"sample_layer{layer}", input_output_aliases=aliases,
        compiler_params=pltpu.CompilerParams(
            dimension_semantics=("arbitrary",), vmem_limit_bytes=VMEM_LIMIT),
    )(*operands)


def kernel(x_prompt, x_sample, state_pool, state_conv, cache_mem_k, cache_mem_v, mem_prompt,
           norm_pre, norm_post, mem_norm, w_mem_kv, w_in, pool_w, pool_scale, conv_w, conv_b,
           conv_ln_g, conv_ln_b, w_branch, w_out):
    W = BRANCH_W
    nb, nt, D = x_sample.shape
    row = lambda p: p.reshape(DEPTH, 1, p.shape[-1])
    gpre, gpost, gmem = row(norm_pre), row(norm_post), row(mem_norm)
    pscale, convb, lng, lnb = row(pool_scale), row(conv_b), row(conv_ln_g), row(conv_ln_b)
    ck = cache_mem_k.reshape(DEPTH, nb, N_MEM * N_XHEADS, XHEAD_DIM)
    cv = cache_mem_v.reshape(DEPTH, nb, N_MEM * N_XHEADS, XHEAD_DIM)

    spool_t = state_pool.transpose(0, 2, 1, 3)
    sconv_t = state_conv.transpose(0, 2, 1, 3)
    xs = x_sample

    xp = x_prompt
    prompt_new, sample_new = None, None
    for i in range(DEPTH):
        xp, *prompt_new = _prompt_layer(
            xp, mem_prompt, i, prompt_new, gpre, gpost, gmem, w_mem_kv, w_in, pool_w, pscale, conv_w,
            convb, lng, lnb, w_branch, w_out)
        win, wbr, wout = prompt_new[-3:]
        prompt_new = prompt_new[:-3]
        xs, *sample_new = _sample_layer(
            xs, spool_t, sconv_t, ck, cv, i, sample_new, gpre, gpost, win, pool_w, pscale, conv_w,
            convb, lng, lnb, wbr, wout)
    pool_p, conv_p, mk_p, mv_p = prompt_new
    pool_s, conv_s = (st.transpose(0, 2, 1, 3) for st in sample_new)
    heads = lambda kv: kv.reshape(DEPTH, kv.shape[1], N_MEM, N_XHEADS, XHEAD_DIM)
    return (xp, xs, pool_p, conv_p, heads(mk_p), heads(mv_p), pool_s, conv_s)
```

```python
import functools
import math

import jax
import jax.numpy as jnp
from jax import lax
from jax.experimental import pallas as pl
from jax.experimental.pallas import tpu as pltpu

D_MODEL = 1024
DEPTH = 2
DEC_SEQ = 8
PAST_LEN = 16384
BRANCH_W = D_MODEL // 2
N_BRANCH = 3
POOL_WINDOWS = (2, 4, 8, 16)
POOL_GROUP_W = BRANCH_W // len(POOL_WINDOWS)
POOL_BUF = max(POOL_WINDOWS) - 1
CONV_WIDTH = 31
CONV_BUF = CONV_WIDTH - 1
N_MEM = 256
N_XHEADS = 4
XHEAD_DIM = BRANCH_W // N_XHEADS
N_IN_SLICES = 7
EPS = 1e-6

LANES = 128
SUBLANES = 8
POOL_PAD = 8
POOL_HEAD = POOL_PAD + 16
CONV_HEAD = 32
PROMPT_TILE = 256
SAMPLE_SEQS = 16
WIDE_STAGE_ROWS = 128
VMEM_LIMIT = 60 * 1024 * 1024
NEG_LOG2E = -1.0 / math.log(2.0)
MASKED_SCORE = float("-inf")

F32 = jnp.float32
BF16 = jnp.bfloat16


def _sigmoid(x):
    return 1.0 / (1.0 + jnp.exp2(x * NEG_LOG2E))


def _silu(x):
    return x * _sigmoid(x)


def _rmsnorm(x, g):
    return x * lax.rsqrt(jnp.mean(x * x, axis=-1, keepdims=True) + EPS) * g


def _dot(a, b):
    return jnp.dot(a, b, preferred_element_type=F32)


def _pool_finish(acc_parts, p_in, p_gate, cnt_parts, poolw_ref, pscale_ref):
    ys = []
    for g in range(len(POOL_WINDOWS)):
        c0 = g * POOL_GROUP_W
        mixed = acc_parts[g] / cnt_parts[g] - p_in[:, c0:c0 + POOL_GROUP_W]
        ys.append(_dot(mixed.astype(BF16), poolw_ref[g].astype(BF16)))
    y = jnp.concatenate(ys, axis=-1)
    return y * pscale_ref[...] * _silu(p_gate)


def _conv_finish(cv, c_gate, convb_ref, lng_ref, lnb_ref):
    cv = cv + convb_ref[...]
    mu = jnp.mean(cv, axis=-1, keepdims=True)
    d = cv - mu
    var = jnp.mean(d * d, axis=-1, keepdims=True)
    r = d * lax.rsqrt(var + EPS) * lng_ref[...] + lnb_ref[...]
    return _silu(r) * _silu(c_gate)


def _merge_out(x, h, branches, win_ref, wbr_ref, wout_ref, gpost_ref):
    merged = None
    base = N_IN_SLICES * BRANCH_W
    for n, br in enumerate(branches):
        proj = _dot(br.astype(BF16), wbr_ref[n])
        gate = _sigmoid(_dot(h, win_ref[:, base + n * D_MODEL:base + (n + 1) * D_MODEL]))
        term = gate * proj
        merged = term if merged is None else merged + term
    y = _dot(merged.astype(BF16), wout_ref[...])
    return x + _rmsnorm(y, gpost_ref[...])


def _layer_row(ref, layer):
    return ref.at[pl.ds(layer, 1)]


def _own_layer(ref, layer, first):
    return ref.at[layer] if first else ref


def _zero_other_layers(ref, layer):
    for other in range(ref.shape[0]):
        if other != layer:
            ref[other] = jnp.zeros(ref.shape[1:], ref.dtype)


def _stage_and_cast(streams):
    copies = [[pltpu.make_async_copy(src, stage.at[i % 2], sem.at[i % 2])
               for i, (src, _) in enumerate(chunks)] for chunks, stage, sem in streams]
    for stream in copies:
        stream[0].start()
    for i in range(max(len(stream) for stream in copies)):
        for stream in copies:
            if i + 1 < len(stream):
                stream[i + 1].start()
        for stream, (chunks, stage, _) in zip(copies, streams):
            if i < len(stream):
                stream[i].wait()
                chunks[i][1][...] = stage[i % 2].astype(BF16)


def _prompt_kernel(x_ref, mem_ref, gpre_ref, gpost_ref, gmem_ref, wkv_hbm, win_hbm, poolw_ref,
                   pscale_ref, convw_ref, convb_ref, lng_ref, lnb_ref, wbr_hbm, wout_hbm,
                   y_ref, npool_ref, nconv_ref, mk_ref, mv_ref, win_out, wbr_out, wout_out,
                   pool_ext, pool_tmp, conv_ext, conv_sh, gate_s, kt_s, v_s,
                   wkv_ref, win_ref, wbr_ref, wout_ref, stage_wide, stage_narrow, sem_wide, sem_narrow,
                   sem_out, *, ts, layer, first):
    W = BRANCH_W
    s = pl.program_id(1)
    first_step = jnp.logical_and(pl.program_id(0) == 0, s == 0)
    gpre_ref, gpost_ref, gmem_ref, pscale_ref, convb_ref, lng_ref, lnb_ref = (
        _layer_row(r, layer) for r in (gpre_ref, gpost_ref, gmem_ref, pscale_ref, convb_ref, lng_ref, lnb_ref))
    per_layer = (npool_ref, nconv_ref, mk_ref, mv_ref)
    npool_ref, nconv_ref, mk_ref, mv_ref = (_own_layer(r, layer, first) for r in per_layer)

    def hand_over():
        return [pltpu.make_async_copy(src, dst, sem_out.at[i]) for i, (src, dst) in enumerate(
            ((win_ref, win_out), (wbr_ref, wbr_out), (wout_ref, wout_out)))]

    @pl.when(first_step)
    def _():
        nrows = stage_narrow.shape[1]
        narrow = [(wkv_hbm.at[layer, pl.ds(r, nrows), :], wkv_ref.at[pl.ds(r, nrows), :])
                  for r in range(0, D_MODEL, nrows)]
        narrow += [(wbr_hbm.at[layer, n], wbr_ref.at[n]) for n in range(N_BRANCH)]
        narrow += [(wout_hbm.at[layer, pl.ds(r, nrows), :], wout_ref.at[pl.ds(r, nrows), :])
                   for r in range(0, D_MODEL, nrows)]
        wrows = stage_wide.shape[1]
        wide = [(win_hbm.at[layer, pl.ds(r, wrows), :], win_ref.at[pl.ds(r, wrows), :])
                for r in range(0, D_MODEL, wrows)]
        _stage_and_cast([(narrow, stage_narrow, sem_narrow), (wide, stage_wide, sem_wide)])
        for copy in hand_over():
            copy.start()

    @pl.when(s == 0)
    def _():
        if first:
            for r in per_layer:
                _zero_other_layers(r, layer)
        pool_ext[0:POOL_HEAD, :] = jnp.zeros((POOL_HEAD, W), F32)
        pool_tmp[:, 0:POOL_PAD, :] = jnp.zeros((pool_tmp.shape[0], POOL_PAD, POOL_GROUP_W), F32)
        conv_ext[0:CONV_HEAD, :] = jnp.zeros((CONV_HEAD, W), F32)
        hm = _rmsnorm(mem_ref[0], gmem_ref[...]).astype(BF16)
        kv = _dot(hm, wkv_ref[...])
        k = kv[:, :W]
        v = kv[:, W:]
        for hd in range(N_XHEADS):
            rows = pl.ds(hd, N_MEM, stride=N_XHEADS)
            mk_ref[0, rows, :] = k[:, hd * XHEAD_DIM:(hd + 1) * XHEAD_DIM]
            mv_ref[0, rows, :] = v[:, hd * XHEAD_DIM:(hd + 1) * XHEAD_DIM]
        kt_s[...] = k.T.astype(BF16)
        v_s[...] = v.astype(BF16)

    x = x_ref[0]
    h = _rmsnorm(x, gpre_ref[...]).astype(BF16)

    pb = _dot(h, win_ref[:, 2 * W:5 * W])
    u = pb[:, :W] * _sigmoid(pb[:, W:2 * W])
    c_gate = pb[:, 2 * W:]
    conv_ext[CONV_HEAD:CONV_HEAD + ts, :] = u
    for r in range(1, SUBLANES):
        conv_sh[r - 1, :, :] = conv_ext[r:r + ts + CONV_HEAD - SUBLANES, :]

    off = CONV_HEAD - CONV_BUF
    gate_base = N_IN_SLICES * W
    n_slabs = W // LANES
    gate_chunk = N_BRANCH * D_MODEL // n_slabs
    slabs = []
    for c in range(n_slabs):
        g0 = c * gate_chunk
        gate_s[:, g0:g0 + gate_chunk] = _sigmoid(
            _dot(h, win_ref[:, gate_base + g0:gate_base + g0 + gate_chunk]))
        c0 = c * LANES
        acc = None
        for k in range(CONV_WIDTH):
            a8, r = divmod(off + k, SUBLANES)
            if r == 0:
                rows = conv_ext[a8 * SUBLANES:a8 * SUBLANES + ts, c0:c0 + LANES]
            else:
                rows = conv_sh[r - 1, a8 * SUBLANES:a8 * SUBLANES + ts, c0:c0 + LANES]
            term = rows * convw_ref[k:k + 1, c0:c0 + LANES]
            acc = term if acc is None else acc + term
        slabs.append(acc)
    bconv = _conv_finish(jnp.concatenate(slabs, axis=-1), c_gate, convb_ref, lng_ref, lnb_ref)
    nconv_ref[0] = conv_ext[CONV_HEAD + ts - CONV_BUF:CONV_HEAD + ts, :]
    conv_ext[0:CONV_HEAD, :] = conv_ext[ts:ts + CONV_HEAD, :]

    pa = _dot(h, win_ref[:, 0:2 * W])
    p_in = pa[:, :W]
    p_gate = pa[:, W:]
    pool_ext[POOL_HEAD:POOL_HEAD + ts, :] = p_in
    pos1 = (lax.broadcasted_iota(jnp.int32, (ts, POOL_GROUP_W), 0) + s * ts + 1).astype(F32)
    accs, cnts = [], []
    for g, win in enumerate(POOL_WINDOWS):
        c0 = g * POOL_GROUP_W
        level, shift = None, 1
        while shift < win:
            last = 2 * shift == win
            lo = POOL_HEAD if last else SUBLANES
            n = POOL_HEAD + ts - lo
            if level is None:
                cur = (pool_ext[lo:lo + n, c0:c0 + POOL_GROUP_W]
                       + pool_ext[lo - shift:lo - shift + n, c0:c0 + POOL_GROUP_W])
            else:
                cur = pool_tmp[level, lo:lo + n, :] + pool_tmp[level, lo - shift:lo - shift + n, :]
            if last:
                accs.append(cur)
            else:
                level = 0 if level is None else level + 1
                pool_tmp[level, lo:lo + n, :] = cur
            shift *= 2
        cnts.append(jnp.minimum(pos1, float(win)))
    a = _pool_finish(accs, p_in, p_gate, cnts, poolw_ref, pscale_ref)
    npool_ref[0] = pool_ext[POOL_HEAD + ts - POOL_BUF:POOL_HEAD + ts, :]
    pool_ext[POOL_PAD:POOL_HEAD, :] = pool_ext[ts + POOL_PAD:ts + POOL_HEAD, :]

    pc = _dot(h, win_ref[:, 5 * W:7 * W])
    q = pc[:, :W]
    x_gate = pc[:, W:]
    scale = 1.0 / math.sqrt(XHEAD_DIM)
    outs = []
    for hd in range(N_XHEADS):
        c0 = hd * XHEAD_DIM
        sc = _dot((q[:, c0:c0 + XHEAD_DIM] * scale).astype(BF16), kt_s[c0:c0 + XHEAD_DIM, :])
        e = jnp.exp(sc - jnp.max(sc, axis=-1, keepdims=True))
        l = jnp.sum(e, axis=-1, keepdims=True)
        outs.append(_dot(e.astype(BF16), v_s[:, c0:c0 + XHEAD_DIM]) / l)
    cattn = jnp.concatenate(outs, axis=-1) * _silu(x_gate)

    merged = None
    for n, br in enumerate((a, bconv, cattn)):
        term = gate_s[:, n * D_MODEL:(n + 1) * D_MODEL] * _dot(br.astype(BF16), wbr_ref[n])
        merged = term if merged is None else merged + term
    y = _dot(merged.astype(BF16), wout_ref[...])
    y_ref[0] = x + _rmsnorm(y, gpost_ref[...])

    @pl.when(first_step)
    def _():
        for copy in hand_over():
            copy.wait()


def _sample_kernel(x_ref, spool_ref, sconv_ref, k_ref, v_ref, gpre_ref, gpost_ref, win_ref,
                   poolw_ref, pscale_ref, convw_ref, convb_ref, lng_ref, lnb_ref, wbr_ref, wout_ref,
                   y_ref, npool_ref, nconv_ref, pool_ext, conv_ext, q_s, o_s, xy_s, v_buf, v_sem,
                   *, nseq, layer, first):
    W = BRANCH_W
    T = nseq * DEC_SEQ
    n_slabs = W // LANES
    step = pl.program_id(0)
    gpre_ref, gpost_ref, pscale_ref, convb_ref, lng_ref, lnb_ref = (
        _layer_row(r, layer) for r in (gpre_ref, gpost_ref, pscale_ref, convb_ref, lng_ref, lnb_ref))
    if first:
        _zero_other_layers(npool_ref, layer)
        _zero_other_layers(nconv_ref, layer)
    npool_ref = _own_layer(npool_ref, layer, first)
    nconv_ref = _own_layer(nconv_ref, layer, first)

    def v_copy(block):
        return pltpu.make_async_copy(v_ref.at[layer, pl.ds(block * nseq, nseq)], v_buf, v_sem)

    @pl.when(step == 0)
    def _():
        v_copy(0).start()

    x_seq = x_ref[...].reshape(T, D_MODEL)
    for c in range(D_MODEL // LANES):
        xy_s[c] = x_seq[:, c * LANES:(c + 1) * LANES]
    x = jnp.concatenate([
        jnp.concatenate([xy_s[c, pl.ds(t, nseq, stride=DEC_SEQ), :] for c in range(D_MODEL // LANES)], axis=-1)
        for t in range(DEC_SEQ)], axis=0)
    h = _rmsnorm(x, gpre_ref[...]).astype(BF16)

    pa = _dot(h, win_ref[:, 0:2 * W])
    p_in = pa[:, :W]
    p_gate = pa[:, W:]
    pos1 = (lax.broadcasted_iota(jnp.int32, (DEC_SEQ, nseq, POOL_GROUP_W), 0) + PAST_LEN + 1).astype(F32)
    pos1 = pos1.reshape(T, POOL_GROUP_W)
    accs, cnts = [], []
    for g, win in enumerate(POOL_WINDOWS):
        c0 = g * POOL_GROUP_W
        pool_ext[g, 0:POOL_BUF * nseq, :] = spool_ref[:, :, c0:c0 + POOL_GROUP_W].reshape(
            POOL_BUF * nseq, POOL_GROUP_W)
        pool_ext[g, POOL_BUF * nseq:(POOL_BUF + DEC_SEQ) * nseq, :] = p_in[:, c0:c0 + POOL_GROUP_W]
        acc = None
        for i in range(win):
            r0 = (POOL_BUF - i) * nseq
            rows = pool_ext[g, r0:r0 + T, :]
            acc = rows if acc is None else acc + rows
        accs.append(acc)
        cnts.append(jnp.minimum(pos1, float(win)))
    a = _pool_finish(accs, p_in, p_gate, cnts, poolw_ref, pscale_ref)
    for g in range(n_slabs):
        npool_ref[:, :, g * LANES:(g + 1) * LANES] = pool_ext[
            g, DEC_SEQ * nseq:(DEC_SEQ + POOL_BUF) * nseq, :].reshape(POOL_BUF, nseq, LANES)

    pb = _dot(h, win_ref[:, 2 * W:5 * W])
    u = pb[:, :W] * _sigmoid(pb[:, W:2 * W])
    c_gate = pb[:, 2 * W:]
    slabs = []
    for c in range(n_slabs):
        c0 = c * LANES
        conv_ext[c, 0:CONV_BUF * nseq, :] = sconv_ref[:, :, c0:c0 + LANES].reshape(CONV_BUF * nseq, LANES)
        conv_ext[c, CONV_BUF * nseq:(CONV_BUF + DEC_SEQ) * nseq, :] = u[:, c0:c0 + LANES]
        acc = None
        for k in range(CONV_WIDTH):
            term = conv_ext[c, k * nseq:k * nseq + T, :] * convw_ref[k:k + 1, c0:c0 + LANES]
            acc = term if acc is None else acc + term
        slabs.append(acc)
    bconv = _conv_finish(jnp.concatenate(slabs, axis=-1), c_gate, convb_ref, lng_ref, lnb_ref)
    for c in range(n_slabs):
        nconv_ref[:, :, c * LANES:(c + 1) * LANES] = conv_ext[
            c, DEC_SEQ * nseq:(DEC_SEQ + CONV_BUF) * nseq, :].reshape(CONV_BUF, nseq, LANES)

    pc = _dot(h, win_ref[:, 5 * W:7 * W])
    x_gate = pc[:, W:]
    for hd in range(N_XHEADS):
        q_s[hd] = pc[:, hd * XHEAD_DIM:(hd + 1) * XHEAD_DIM]
    scale = 1.0 / math.sqrt(XHEAD_DIM)
    nq = N_XHEADS * DEC_SEQ
    nk = N_MEM * N_XHEADS
    q4 = jnp.stack([
        jnp.concatenate([q_s[hd, pl.ds(sq, DEC_SEQ, stride=nseq), :] for hd in range(N_XHEADS)], axis=0)
        for sq in range(nseq)]).astype(BF16)
    sc = jnp.einsum('gqd,gkd->gqk', q4, k_ref[...].astype(BF16), preferred_element_type=F32) * scale
    q_head = lax.broadcasted_iota(jnp.int32, (nq, nk), 0) // DEC_SEQ
    k_head = lax.broadcasted_iota(jnp.int32, (nq, nk), 1) % N_XHEADS
    sc = jnp.where((q_head == k_head)[None], sc, MASKED_SCORE)
    e = jnp.exp(sc - jnp.max(sc, axis=-1, keepdims=True))
    l = jnp.sum(e, axis=-1, keepdims=True)
    v_copy(step).wait()
    o = jnp.einsum('gqk,gkd->gqd', e.astype(BF16), v_buf[...].astype(BF16),
                   preferred_element_type=F32) / l
    for sq in range(nseq):
        for hd in range(N_XHEADS):
            o_s[hd, pl.ds(sq, DEC_SEQ, stride=nseq), :] = o[sq, hd * DEC_SEQ:(hd + 1) * DEC_SEQ, :]
    cattn = jnp.concatenate([o_s[hd] for hd in range(N_XHEADS)], axis=-1) * _silu(x_gate)

    y = _merge_out(x, h, (a, bconv, cattn), win_ref, wbr_ref, wout_ref, gpost_ref)
    for c in range(D_MODEL // LANES):
        xy_s[c] = y[:, c * LANES:(c + 1) * LANES]
    for sq in range(nseq):
        y_ref[sq] = jnp.concatenate(
            [xy_s[c, pl.ds(sq, DEC_SEQ, stride=nseq), :] for c in range(D_MODEL // LANES)], axis=-1)

    @pl.when(step + 1 < pl.num_programs(0))
    def _():
        v_copy(step + 1).start()


def _layer_spec(shape, layer):
    zeros = (0,) * len(shape)
    return pl.BlockSpec((None,) + shape, lambda *_: (layer,) + zeros, pipeline_mode=pl.Buffered(1))


def _whole_spec(shape):
    zeros = (0,) * len(shape)
    return pl.BlockSpec(shape, lambda *_: zeros, pipeline_mode=pl.Buffered(1))


def _carry_layers(operands, in_specs, carried, first_output):
    aliases = {}
    for i, arr in enumerate(carried or ()):
        aliases[len(operands)] = first_output + i
        operands.append(arr)
        in_specs.append(pl.BlockSpec(memory_space=pl.ANY))
    return aliases


def _per_layer_spec(block, index_map, layer, first):
    if first:
        return pl.BlockSpec((DEPTH,) + block, lambda *i: (0,) + index_map(*i))
    return pl.BlockSpec((None,) + block, lambda *i: (layer,) + index_map(*i))


def _skip_carried(body, n_inputs, n_carried):
    def wrapped(*refs, **kw):
        return body(*refs[:n_inputs], *refs[n_inputs + n_carried:], **kw)
    return wrapped


def _prompt_layer(x, mem, layer, carried, gpre, gpost, gmem, wkv, win, poolw, pscale, convw, convb,
                  lng, lnb, wbr, wout):
    B, S, D = x.shape
    W = BRANCH_W
    ts = PROMPT_TILE
    grid = (B, S // ts)
    small = lambda width: _whole_spec((DEPTH, width))
    in_hbm = pl.BlockSpec(memory_space=pl.ANY)
    in_specs = [
        pl.BlockSpec((1, ts, D), lambda b, s: (b, s, 0)),
        pl.BlockSpec((1, N_MEM, D), lambda b, s: (b, 0, 0)),
        small(D), small(D), small(D),
        in_hbm, in_hbm,
        _layer_spec(poolw.shape[1:], layer),
        small(W),
        _layer_spec((CONV_WIDTH, W), layer),
        small(W), small(W), small(W),
        in_hbm, in_hbm,
    ]
    per_layer = [(POOL_BUF, W), (CONV_BUF, W), (N_MEM * N_XHEADS, XHEAD_DIM), (N_MEM * N_XHEADS, XHEAD_DIM)]
    first = carried is None
    out_specs = [pl.BlockSpec((1, ts, D), lambda b, s: (b, s, 0))] + [
        _per_layer_spec((1,) + shape, lambda b, s: (b, 0, 0), layer, first) for shape in per_layer]
    out_shape = [jax.ShapeDtypeStruct((B, S, D), F32)] + [
        jax.ShapeDtypeStruct((DEPTH, B) + shape, F32) for shape in per_layer]
    handed = [win.shape[1:], wbr.shape[1:], wout.shape[1:]]
    out_specs += [pl.BlockSpec(memory_space=pl.ANY)] * len(handed)
    out_shape += [jax.ShapeDtypeStruct(shape, BF16) for shape in handed]
    operands = [x, mem, gpre, gpost, gmem, wkv, win, poolw, pscale, convw, convb, lng, lnb, wbr, wout]
    aliases = _carry_layers(operands, in_specs, carried, first_output=1)
    scratch = [
        pltpu.VMEM((POOL_HEAD + ts, W), F32),
        pltpu.VMEM((len(POOL_WINDOWS) - 1, POOL_HEAD + ts, POOL_GROUP_W), F32),
        pltpu.VMEM((CONV_HEAD + ts, W), F32),
        pltpu.VMEM((SUBLANES - 1, ts + CONV_HEAD - SUBLANES, W), F32),
        pltpu.VMEM((ts, N_BRANCH * D_MODEL), F32),
        pltpu.VMEM((W, N_MEM), BF16),
        pltpu.VMEM((N_MEM, W), BF16),
        pltpu.VMEM(wkv.shape[1:], BF16),
        pltpu.VMEM(win.shape[1:], BF16),
        pltpu.VMEM(wbr.shape[1:], BF16),
        pltpu.VMEM(wout.shape[1:], BF16),
        pltpu.VMEM((2, WIDE_STAGE_ROWS, win.shape[-1]), F32),
        pltpu.VMEM((2, W, D), F32),
        pltpu.SemaphoreType.DMA((2,)),
        pltpu.SemaphoreType.DMA((2,)),
        pltpu.SemaphoreType.DMA((len(handed),)),
    ]
    body = _skip_carried(_prompt_kernel, len(operands) - len(aliases), len(aliases))
    return pl.pallas_call(
        functools.partial(body, ts=ts, layer=layer, first=first),
        grid=grid, in_specs=in_specs, out_specs=out_specs, out_shape=out_shape,
        scratch_shapes=scratch, name=f"prompt_layer{layer}", input_output_aliases=aliases,
        compiler_params=pltpu.CompilerParams(
            dimension_semantics=("arbitrary", "arbitrary"), vmem_limit_bytes=VMEM_LIMIT),
    )(*operands)


def _sample_layer(x, spool, sconv, ck, cv, layer, carried, gpre, gpost, win, poolw, pscale, convw, convb,
                  lng, lnb, wbr, wout):
    nb, nt, D = x.shape
    W = BRANCH_W
    nseq = SAMPLE_SEQS
    T = nseq * nt
    n_slabs = W // LANES
    small = lambda width: _whole_spec((DEPTH, width))
    state_in = lambda rows: pl.BlockSpec((None, rows, nseq, W), lambda g: (layer, 0, g, 0))
    cache = pl.BlockSpec((None, nseq, N_MEM * N_XHEADS, XHEAD_DIM), lambda g: (layer, g, 0, 0))
    in_specs = [
        pl.BlockSpec((nseq, nt, D), lambda g: (g, 0, 0)),
        state_in(POOL_BUF), state_in(CONV_BUF), cache, pl.BlockSpec(memory_space=pl.ANY),
        small(D), small(D),
        _whole_spec(win.shape),
        _layer_spec(poolw.shape[1:], layer),
        small(W),
        _layer_spec((CONV_WIDTH, W), layer),
        small(W), small(W), small(W),
        _whole_spec(wbr.shape),
        _whole_spec(wout.shape),
    ]
    first = carried is None
    out_specs = [pl.BlockSpec((nseq, nt, D), lambda g: (g, 0, 0))] + [
        _per_layer_spec((rows, nseq, W), lambda g: (0, g, 0), layer, first) for rows in (POOL_BUF, CONV_BUF)]
    out_shape = [jax.ShapeDtypeStruct((nb, nt, D), F32)] + [
        jax.ShapeDtypeStruct((DEPTH, rows, nb, W), F32) for rows in (POOL_BUF, CONV_BUF)]
    operands = [x, spool, sconv, ck, cv, gpre, gpost, win, poolw, pscale, convw, convb, lng, lnb, wbr, wout]
    aliases = _carry_layers(operands, in_specs, carried, first_output=1)
    scratch = [
        pltpu.VMEM((n_slabs, (POOL_BUF + nt) * nseq, LANES), F32),
        pltpu.VMEM((n_slabs, (CONV_BUF + nt) * nseq, LANES), F32),
        pltpu.VMEM((N_XHEADS, T, XHEAD_DIM), F32),
        pltpu.VMEM((N_XHEADS, T, XHEAD_DIM), F32),
        pltpu.VMEM((D // LANES, T, LANES), F32),
        pltpu.VMEM((nseq, N_MEM * N_XHEADS, XHEAD_DIM), F32),
        pltpu.SemaphoreType.DMA(()),
    ]
    body = _skip_carried(_sample_kernel, len(operands) - len(aliases), len(aliases))
    return pl.pallas_call(
        functools.partial(body, nseq=nseq, layer=layer, first=first),
        grid=(nb // nseq,), in_specs=in_specs, out_specs=out_specs, out_shape=out_shape,
        scratch_shapes=scratch, name=f"sample_layer{layer}", input_output_aliases=aliases,
        compiler_params=pltpu.CompilerParams(
            dimension_semantics=("arbitrary",), vmem_limit_bytes=VMEM_LIMIT),
    )(*operands)


def kernel(x_prompt, x_sample, state_pool, state_conv, cache_mem_k, cache_mem_v, mem_prompt,
           norm_pre, norm_post, mem_norm, w_mem_kv, w_in, pool_w, pool_scale, conv_w, conv_b,
           conv_ln_g, conv_ln_b, w_branch, w_out):
    W = BRANCH_W
    nb, nt, D = x_sample.shape
    gpre, gpost, gmem = norm_pre, norm_post, mem_norm
    pscale, convb, lng, lnb = pool_scale, conv_b, conv_ln_g, conv_ln_b
    ck = cache_mem_k.reshape(DEPTH, nb, N_MEM * N_XHEADS, XHEAD_DIM)
    cv = cache_mem_v.reshape(DEPTH, nb, N_MEM * N_XHEADS, XHEAD_DIM)

    spool_t = state_pool.transpose(0, 2, 1, 3)
    sconv_t = state_conv.transpose(0, 2, 1, 3)
    xs = x_sample

    xp = x_prompt
    prompt_new, sample_new = None, None
    for i in range(DEPTH):
        xp, *prompt_new = _prompt_layer(
            xp, mem_prompt, i, prompt_new, gpre, gpost, gmem, w_mem_kv, w_in, pool_w, pscale, conv_w,
            convb, lng, lnb, w_branch, w_out)
        win, wbr, wout = prompt_new[-3:]
        prompt_new = prompt_new[:-3]
        xs, *sample_new = _sample_layer(
            xs, spool_t, sconv_t, ck, cv, i, sample_new, gpre, gpost, win, pool_w, pscale, conv_w,
            convb, lng, lnb, wbr, wout)
    pool_p, conv_p, mk_p, mv_p = prompt_new
    pool_s, conv_s = (st.transpose(0, 2, 1, 3) for st in sample_new)
    heads = lambda kv: kv.reshape(DEPTH, kv.shape[1], N_MEM, N_XHEADS, XHEAD_DIM)
    return (xp, xs, pool_p, conv_p, heads(mk_p), heads(mv_p), pool_s, conv_s)
```

```python
import functools
import math

import jax
import jax.numpy as jnp
from jax import lax
from jax.experimental import pallas as pl
from jax.experimental.pallas import tpu as pltpu

D_MODEL = 1024
DEPTH = 2
DEC_SEQ = 8
PAST_LEN = 16384
BRANCH_W = D_MODEL // 2
N_BRANCH = 3
POOL_WINDOWS = (2, 4, 8, 16)
POOL_GROUP_W = BRANCH_W // len(POOL_WINDOWS)
POOL_BUF = max(POOL_WINDOWS) - 1
CONV_WIDTH = 31
CONV_BUF = CONV_WIDTH - 1
N_MEM = 256
N_XHEADS = 4
XHEAD_DIM = BRANCH_W // N_XHEADS
N_IN_SLICES = 7
EPS = 1e-6

LANES = 128
SUBLANES = 8
POOL_PAD = 8
POOL_HEAD = POOL_PAD + 16
CONV_HEAD = 32
PROMPT_TILE = 256
SAMPLE_SEQS = 16
STAGE_SLOTS = 4
WIDE_STAGE_ROWS = 64
NARROW_STAGE_ROWS = 256
VMEM_LIMIT = 60 * 1024 * 1024
NEG_LOG2E = -1.0 / math.log(2.0)
MASKED_SCORE = float("-inf")

F32 = jnp.float32
BF16 = jnp.bfloat16


def _sigmoid(x):
    return 1.0 / (1.0 + jnp.exp2(x * NEG_LOG2E))


def _silu(x):
    return x * _sigmoid(x)


def _rmsnorm(x, g):
    return x * lax.rsqrt(jnp.mean(x * x, axis=-1, keepdims=True) + EPS) * g


def _dot(a, b):
    return jnp.dot(a, b, preferred_element_type=F32)


def _pool_finish(acc_parts, p_in, p_gate, cnt_parts, poolw_ref, pscale_ref):
    ys = []
    for g in range(len(POOL_WINDOWS)):
        c0 = g * POOL_GROUP_W
        mixed = acc_parts[g] / cnt_parts[g] - p_in[:, c0:c0 + POOL_GROUP_W]
        ys.append(_dot(mixed.astype(BF16), poolw_ref[g].astype(BF16)))
    y = jnp.concatenate(ys, axis=-1)
    return y * pscale_ref[...] * _silu(p_gate)


def _conv_finish(cv, c_gate, convb_ref, lng_ref, lnb_ref):
    cv = cv + convb_ref[...]
    mu = jnp.mean(cv, axis=-1, keepdims=True)
    d = cv - mu
    var = jnp.mean(d * d, axis=-1, keepdims=True)
    r = d * lax.rsqrt(var + EPS) * lng_ref[...] + lnb_ref[...]
    return _silu(r) * _silu(c_gate)


def _merge_out(x, h, branches, win_ref, wbr_ref, wout_ref, gpost_ref):
    merged = None
    base = N_IN_SLICES * BRANCH_W
    for n, br in enumerate(branches):
        proj = _dot(br.astype(BF16), wbr_ref[n])
        gate = _sigmoid(_dot(h, win_ref[:, base + n * D_MODEL:base + (n + 1) * D_MODEL]))
        term = gate * proj
        merged = term if merged is None else merged + term
    y = _dot(merged.astype(BF16), wout_ref[...])
    return x + _rmsnorm(y, gpost_ref[...])


def _layer_row(ref, layer):
    return ref.at[pl.ds(layer, 1)]


def _own_layer(ref, layer, first):
    return ref.at[layer] if first else ref


def _zero_other_layers(ref, layer):
    for other in range(ref.shape[0]):
        if other != layer:
            ref[other] = jnp.zeros(ref.shape[1:], ref.dtype)


def _stage_and_cast(streams):
    copies = [[pltpu.make_async_copy(src, stage.at[i % stage.shape[0]], sem.at[i % stage.shape[0]])
               for i, (src, _) in enumerate(chunks)] for chunks, stage, sem in streams]
    for stream, (_, stage, _) in zip(copies, streams):
        for copy in stream[:stage.shape[0] - 1]:
            copy.start()
    for i in range(max(len(stream) for stream in copies)):
        for stream, (_, stage, _) in zip(copies, streams):
            ahead = i + stage.shape[0] - 1
            if ahead < len(stream):
                stream[ahead].start()
        for stream, (chunks, stage, _) in zip(copies, streams):
            if i < len(stream):
                stream[i].wait()
                chunks[i][1][...] = stage[i % stage.shape[0]].astype(BF16)


def _prompt_kernel(x_ref, mem_ref, gpre_ref, gpost_ref, gmem_ref, wkv_hbm, win_hbm, poolw_ref,
                   pscale_ref, convw_ref, convb_ref, lng_ref, lnb_ref, wbr_hbm, wout_hbm,
                   y_ref, npool_ref, nconv_ref, mk_ref, mv_ref, win_out, wbr_out, wout_out,
                   pool_ext, pool_tmp, conv_ext, conv_sh, gate_s, kt_s, v_s,
                   wkv_ref, win_ref, wbr_ref, wout_ref, stage_wide, stage_narrow, sem_wide, sem_narrow,
                   sem_out, *, ts, layer, first):
    W = BRANCH_W
    s = pl.program_id(1)
    first_step = jnp.logical_and(pl.program_id(0) == 0, s == 0)
    gpre_ref, gpost_ref, gmem_ref, pscale_ref, convb_ref, lng_ref, lnb_ref = (
        _layer_row(r, layer) for r in (gpre_ref, gpost_ref, gmem_ref, pscale_ref, convb_ref, lng_ref, lnb_ref))
    per_layer = (npool_ref, nconv_ref, mk_ref, mv_ref)
    npool_ref, nconv_ref, mk_ref, mv_ref = (_own_layer(r, layer, first) for r in per_layer)

    def hand_over():
        return [pltpu.make_async_copy(src, dst, sem_out.at[i]) for i, (src, dst) in enumerate(
            ((win_ref, win_out), (wbr_ref, wbr_out), (wout_ref, wout_out)))]

    @pl.when(first_step)
    def _():
        nrows = stage_narrow.shape[1]
        narrow = [(wkv_hbm.at[layer, pl.ds(r, nrows), :], wkv_ref.at[pl.ds(r, nrows), :])
                  for r in range(0, D_MODEL, nrows)]
        narrow += [(wbr_hbm.at[layer, n, pl.ds(r, nrows), :], wbr_ref.at[n, pl.ds(r, nrows), :])
                   for n in range(N_BRANCH) for r in range(0, W, nrows)]
        narrow += [(wout_hbm.at[layer, pl.ds(r, nrows), :], wout_ref.at[pl.ds(r, nrows), :])
                   for r in range(0, D_MODEL, nrows)]
        wrows = stage_wide.shape[1]
        wide = [(win_hbm.at[layer, pl.ds(r, wrows), :], win_ref.at[pl.ds(r, wrows), :])
                for r in range(0, D_MODEL, wrows)]
        _stage_and_cast([(narrow, stage_narrow, sem_narrow), (wide, stage_wide, sem_wide)])
        for copy in hand_over():
            copy.start()

    @pl.when(s == 0)
    def _():
        if first:
            for r in per_layer:
                _zero_other_layers(r, layer)
        pool_ext[0:POOL_HEAD, :] = jnp.zeros((POOL_HEAD, W), F32)
        pool_tmp[:, 0:POOL_PAD, :] = jnp.zeros((pool_tmp.shape[0], POOL_PAD, POOL_GROUP_W), F32)
        conv_ext[0:CONV_HEAD, :] = jnp.zeros((CONV_HEAD, W), F32)
        hm = _rmsnorm(mem_ref[0], gmem_ref[...]).astype(BF16)
        kv = _dot(hm, wkv_ref[...])
        k = kv[:, :W]
        v = kv[:, W:]
        for hd in range(N_XHEADS):
            rows = pl.ds(hd, N_MEM, stride=N_XHEADS)
            mk_ref[0, rows, :] = k[:, hd * XHEAD_DIM:(hd + 1) * XHEAD_DIM]
            mv_ref[0, rows, :] = v[:, hd * XHEAD_DIM:(hd + 1) * XHEAD_DIM]
        kt_s[...] = k.T.astype(BF16)
        v_s[...] = v.astype(BF16)

    x = x_ref[0]
    h = _rmsnorm(x, gpre_ref[...]).astype(BF16)

    pb = _dot(h, win_ref[:, 2 * W:5 * W])
    u = pb[:, :W] * _sigmoid(pb[:, W:2 * W])
    c_gate = pb[:, 2 * W:]
    conv_ext[CONV_HEAD:CONV_HEAD + ts, :] = u
    for r in range(1, SUBLANES):
        conv_sh[r - 1, :, :] = conv_ext[r:r + ts + CONV_HEAD - SUBLANES, :]

    off = CONV_HEAD - CONV_BUF
    gate_base = N_IN_SLICES * W
    n_slabs = W // LANES
    gate_chunk = N_BRANCH * D_MODEL // n_slabs
    slabs = []
    for c in range(n_slabs):
        g0 = c * gate_chunk
        gate_s[:, g0:g0 + gate_chunk] = _sigmoid(
            _dot(h, win_ref[:, gate_base + g0:gate_base + g0 + gate_chunk]))
        c0 = c * LANES
        acc = None
        for k in range(CONV_WIDTH):
            a8, r = divmod(off + k, SUBLANES)
            if r == 0:
                rows = conv_ext[a8 * SUBLANES:a8 * SUBLANES + ts, c0:c0 + LANES]
            else:
                rows = conv_sh[r - 1, a8 * SUBLANES:a8 * SUBLANES + ts, c0:c0 + LANES]
            term = rows * convw_ref[k:k + 1, c0:c0 + LANES]
            acc = term if acc is None else acc + term
        slabs.append(acc)
    bconv = _conv_finish(jnp.concatenate(slabs, axis=-1), c_gate, convb_ref, lng_ref, lnb_ref)
    nconv_ref[0] = conv_ext[CONV_HEAD + ts - CONV_BUF:CONV_HEAD + ts, :]
    conv_ext[0:CONV_HEAD, :] = conv_ext[ts:ts + CONV_HEAD, :]

    pa = _dot(h, win_ref[:, 0:2 * W])
    p_in = pa[:, :W]
    p_gate = pa[:, W:]
    pool_ext[POOL_HEAD:POOL_HEAD + ts, :] = p_in
    pos1 = (lax.broadcasted_iota(jnp.int32, (ts, POOL_GROUP_W), 0) + s * ts + 1).astype(F32)
    accs, cnts = [], []
    for g, win in enumerate(POOL_WINDOWS):
        c0 = g * POOL_GROUP_W
        level, shift = None, 1
        while shift < win:
            last = 2 * shift == win
            lo = POOL_HEAD if last else SUBLANES
            n = POOL_HEAD + ts - lo
            if level is None:
                cur = (pool_ext[lo:lo + n, c0:c0 + POOL_GROUP_W]
                       + pool_ext[lo - shift:lo - shift + n, c0:c0 + POOL_GROUP_W])
            else:
                cur = pool_tmp[level, lo:lo + n, :] + pool_tmp[level, lo - shift:lo - shift + n, :]
            if last:
                accs.append(cur)
            else:
                level = 0 if level is None else level + 1
                pool_tmp[level, lo:lo + n, :] = cur
            shift *= 2
        cnts.append(jnp.minimum(pos1, float(win)))
    a = _pool_finish(accs, p_in, p_gate, cnts, poolw_ref, pscale_ref)
    npool_ref[0] = pool_ext[POOL_HEAD + ts - POOL_BUF:POOL_HEAD + ts, :]
    pool_ext[POOL_PAD:POOL_HEAD, :] = pool_ext[ts + POOL_PAD:ts + POOL_HEAD, :]

    pc = _dot(h, win_ref[:, 5 * W:7 * W])
    q = pc[:, :W]
    x_gate = pc[:, W:]
    scale = 1.0 / math.sqrt(XHEAD_DIM)
    outs = []
    for hd in range(N_XHEADS):
        c0 = hd * XHEAD_DIM
        sc = _dot((q[:, c0:c0 + XHEAD_DIM] * scale).astype(BF16), kt_s[c0:c0 + XHEAD_DIM, :])
        e = jnp.exp(sc - jnp.max(sc, axis=-1, keepdims=True))
        l = jnp.sum(e, axis=-1, keepdims=True)
        outs.append(_dot(e.astype(BF16), v_s[:, c0:c0 + XHEAD_DIM]) / l)
    cattn = jnp.concatenate(outs, axis=-1) * _silu(x_gate)

    merged = None
    for n, br in enumerate((a, bconv, cattn)):
        term = gate_s[:, n * D_MODEL:(n + 1) * D_MODEL] * _dot(br.astype(BF16), wbr_ref[n])
        merged = term if merged is None else merged + term
    y = _dot(merged.astype(BF16), wout_ref[...])
    y_ref[0] = x + _rmsnorm(y, gpost_ref[...])

    @pl.when(first_step)
    def _():
        for copy in hand_over():
            copy.wait()


def _sample_kernel(x_ref, spool_ref, sconv_ref, k_ref, v_ref, gpre_ref, gpost_ref, win_ref,
                   poolw_ref, pscale_ref, convw_ref, convb_ref, lng_ref, lnb_ref, wbr_ref, wout_ref,
                   y_ref, npool_ref, nconv_ref, pool_ext, conv_ext, q_s, o_s, xy_s, v_buf, v_sem,
                   *, nseq, layer, first):
    W = BRANCH_W
    T = nseq * DEC_SEQ
    n_slabs = W // LANES
    step = pl.program_id(0)
    gpre_ref, gpost_ref, pscale_ref, convb_ref, lng_ref, lnb_ref = (
        _layer_row(r, layer) for r in (gpre_ref, gpost_ref, pscale_ref, convb_ref, lng_ref, lnb_ref))
    if first:
        _zero_other_layers(npool_ref, layer)
        _zero_other_layers(nconv_ref, layer)
    npool_ref = _own_layer(npool_ref, layer, first)
    nconv_ref = _own_layer(nconv_ref, layer, first)

    def v_copy(block):
        return pltpu.make_async_copy(v_ref.at[layer, pl.ds(block * nseq, nseq)], v_buf, v_sem)

    @pl.when(step == 0)
    def _():
        v_copy(0).start()

    x_seq = x_ref[...].reshape(T, D_MODEL)
    for c in range(D_MODEL // LANES):
        xy_s[c] = x_seq[:, c * LANES:(c + 1) * LANES]
    x = jnp.concatenate([
        jnp.concatenate([xy_s[c, pl.ds(t, nseq, stride=DEC_SEQ), :] for c in range(D_MODEL // LANES)], axis=-1)
        for t in range(DEC_SEQ)], axis=0)
    h = _rmsnorm(x, gpre_ref[...]).astype(BF16)

    pa = _dot(h, win_ref[:, 0:2 * W])
    p_in = pa[:, :W]
    p_gate = pa[:, W:]
    pos1 = (lax.broadcasted_iota(jnp.int32, (DEC_SEQ, nseq, POOL_GROUP_W), 0) + PAST_LEN + 1).astype(F32)
    pos1 = pos1.reshape(T, POOL_GROUP_W)
    accs, cnts = [], []
    for g, win in enumerate(POOL_WINDOWS):
        c0 = g * POOL_GROUP_W
        pool_ext[g, 0:POOL_BUF * nseq, :] = spool_ref[:, :, c0:c0 + POOL_GROUP_W].reshape(
            POOL_BUF * nseq, POOL_GROUP_W)
        pool_ext[g, POOL_BUF * nseq:(POOL_BUF + DEC_SEQ) * nseq, :] = p_in[:, c0:c0 + POOL_GROUP_W]
        acc = None
        for i in range(win):
            r0 = (POOL_BUF - i) * nseq
            rows = pool_ext[g, r0:r0 + T, :]
            acc = rows if acc is None else acc + rows
        accs.append(acc)
        cnts.append(jnp.minimum(pos1, float(win)))
    a = _pool_finish(accs, p_in, p_gate, cnts, poolw_ref, pscale_ref)
    for g in range(n_slabs):
        npool_ref[:, :, g * LANES:(g + 1) * LANES] = pool_ext[
            g, DEC_SEQ * nseq:(DEC_SEQ + POOL_BUF) * nseq, :].reshape(POOL_BUF, nseq, LANES)

    pb = _dot(h, win_ref[:, 2 * W:5 * W])
    u = pb[:, :W] * _sigmoid(pb[:, W:2 * W])
    c_gate = pb[:, 2 * W:]
    slabs = []
    for c in range(n_slabs):
        c0 = c * LANES
        conv_ext[c, 0:CONV_BUF * nseq, :] = sconv_ref[:, :, c0:c0 + LANES].reshape(CONV_BUF * nseq, LANES)
        conv_ext[c, CONV_BUF * nseq:(CONV_BUF + DEC_SEQ) * nseq, :] = u[:, c0:c0 + LANES]
        acc = None
        for k in range(CONV_WIDTH):
            term = conv_ext[c, k * nseq:k * nseq + T, :] * convw_ref[k:k + 1, c0:c0 + LANES]
            acc = term if acc is None else acc + term
        slabs.append(acc)
    bconv = _conv_finish(jnp.concatenate(slabs, axis=-1), c_gate, convb_ref, lng_ref, lnb_ref)
    for c in range(n_slabs):
        nconv_ref[:, :, c * LANES:(c + 1) * LANES] = conv_ext[
            c, DEC_SEQ * nseq:(DEC_SEQ + CONV_BUF) * nseq, :].reshape(CONV_BUF, nseq, LANES)

    pc = _dot(h, win_ref[:, 5 * W:7 * W])
    x_gate = pc[:, W:]
    for hd in range(N_XHEADS):
        q_s[hd] = pc[:, hd * XHEAD_DIM:(hd + 1) * XHEAD_DIM]
    scale = 1.0 / math.sqrt(XHEAD_DIM)
    nq = N_XHEADS * DEC_SEQ
    nk = N_MEM * N_XHEADS
    q4 = jnp.stack([
        jnp.concatenate([q_s[hd, pl.ds(sq, DEC_SEQ, stride=nseq), :] for hd in range(N_XHEADS)], axis=0)
        for sq in range(nseq)]).astype(BF16)
    sc = jnp.einsum('gqd,gkd->gqk', q4, k_ref[...].astype(BF16), preferred_element_type=F32) * scale
    q_head = lax.broadcasted_iota(jnp.int32, (nq, nk), 0) // DEC_SEQ
    k_head = lax.broadcasted_iota(jnp.int32, (nq, nk), 1) % N_XHEADS
    sc = jnp.where((q_head == k_head)[None], sc, MASKED_SCORE)
    e = jnp.exp(sc - jnp.max(sc, axis=-1, keepdims=True))
    l = jnp.sum(e, axis=-1, keepdims=True)
    v_copy(step).wait()
    o = jnp.einsum('gqk,gkd->gqd', e.astype(BF16), v_buf[...].astype(BF16),
                   preferred_element_type=F32) / l
    for sq in range(nseq):
        for hd in range(N_XHEADS):
            o_s[hd, pl.ds(sq, DEC_SEQ, stride=nseq), :] = o[sq, hd * DEC_SEQ:(hd + 1) * DEC_SEQ, :]
    cattn = jnp.concatenate([o_s[hd] for hd in range(N_XHEADS)], axis=-1) * _silu(x_gate)

    y = _merge_out(x, h, (a, bconv, cattn), win_ref, wbr_ref, wout_ref, gpost_ref)
    for c in range(D_MODEL // LANES):
        xy_s[c] = y[:, c * LANES:(c + 1) * LANES]
    for sq in range(nseq):
        y_ref[sq] = jnp.concatenate(
            [xy_s[c, pl.ds(sq, DEC_SEQ, stride=nseq), :] for c in range(D_MODEL // LANES)], axis=-1)

    @pl.when(step + 1 < pl.num_programs(0))
    def _():
        v_copy(step + 1).start()


def _layer_spec(shape, layer):
    zeros = (0,) * len(shape)
    return pl.BlockSpec((None,) + shape, lambda *_: (layer,) + zeros, pipeline_mode=pl.Buffered(1))


def _whole_spec(shape):
    zeros = (0,) * len(shape)
    return pl.BlockSpec(shape, lambda *_: zeros, pipeline_mode=pl.Buffered(1))


def _carry_layers(operands, in_specs, carried, first_output):
    aliases = {}
    for i, arr in enumerate(carried or ()):
        aliases[len(operands)] = first_output + i
        operands.append(arr)
        in_specs.append(pl.BlockSpec(memory_space=pl.ANY))
    return aliases


def _per_layer_spec(block, index_map, layer, first):
    if first:
        return pl.BlockSpec((DEPTH,) + block, lambda *i: (0,) + index_map(*i))
    return pl.BlockSpec((None,) + block, lambda *i: (layer,) + index_map(*i))


def _skip_carried(body, n_inputs, n_carried):
    def wrapped(*refs, **kw):
        return body(*refs[:n_inputs], *refs[n_inputs + n_carried:], **kw)
    return wrapped


def _prompt_layer(x, mem, layer, carried, gpre, gpost, gmem, wkv, win, poolw, pscale, convw, convb,
                  lng, lnb, wbr, wout):
    B, S, D = x.shape
    W = BRANCH_W
    ts = PROMPT_TILE
    grid = (B, S // ts)
    small = lambda width: _whole_spec((DEPTH, width))
    in_hbm = pl.BlockSpec(memory_space=pl.ANY)
    in_specs = [
        pl.BlockSpec((1, ts, D), lambda b, s: (b, s, 0)),
        pl.BlockSpec((1, N_MEM, D), lambda b, s: (b, 0, 0)),
        small(D), small(D), small(D),
        in_hbm, in_hbm,
        _layer_spec(poolw.shape[1:], layer),
        small(W),
        _layer_spec((CONV_WIDTH, W), layer),
        small(W), small(W), small(W),
        in_hbm, in_hbm,
    ]
    per_layer = [(POOL_BUF, W), (CONV_BUF, W), (N_MEM * N_XHEADS, XHEAD_DIM), (N_MEM * N_XHEADS, XHEAD_DIM)]
    first = carried is None
    out_specs = [pl.BlockSpec((1, ts, D), lambda b, s: (b, s, 0))] + [
        _per_layer_spec((1,) + shape, lambda b, s: (b, 0, 0), layer, first) for shape in per_layer]
    out_shape = [jax.ShapeDtypeStruct((B, S, D), F32)] + [
        jax.ShapeDtypeStruct((DEPTH, B) + shape, F32) for shape in per_layer]
    handed = [win.shape[1:], wbr.shape[1:], wout.shape[1:]]
    out_specs += [pl.BlockSpec(memory_space=pl.ANY)] * len(handed)
    out_shape += [jax.ShapeDtypeStruct(shape, BF16) for shape in handed]
    operands = [x, mem, gpre, gpost, gmem, wkv, win, poolw, pscale, convw, convb, lng, lnb, wbr, wout]
    aliases = _carry_layers(operands, in_specs, carried, first_output=1)
    scratch = [
        pltpu.VMEM((POOL_HEAD + ts, W), F32),
        pltpu.VMEM((len(POOL_WINDOWS) - 1, POOL_HEAD + ts, POOL_GROUP_W), F32),
        pltpu.VMEM((CONV_HEAD + ts, W), F32),
        pltpu.VMEM((SUBLANES - 1, ts + CONV_HEAD - SUBLANES, W), F32),
        pltpu.VMEM((ts, N_BRANCH * D_MODEL), F32),
        pltpu.VMEM((W, N_MEM), BF16),
        pltpu.VMEM((N_MEM, W), BF16),
        pltpu.VMEM(wkv.shape[1:], BF16),
        pltpu.VMEM(win.shape[1:], BF16),
        pltpu.VMEM(wbr.shape[1:], BF16),
        pltpu.VMEM(wout.shape[1:], BF16),
        pltpu.VMEM((STAGE_SLOTS, WIDE_STAGE_ROWS, win.shape[-1]), F32),
        pltpu.VMEM((STAGE_SLOTS, NARROW_STAGE_ROWS, D), F32),
        pltpu.SemaphoreType.DMA((STAGE_SLOTS,)),
        pltpu.SemaphoreType.DMA((STAGE_SLOTS,)),
        pltpu.SemaphoreType.DMA((len(handed),)),
    ]
    body = _skip_carried(_prompt_kernel, len(operands) - len(aliases), len(aliases))
    return pl.pallas_call(
        functools.partial(body, ts=ts, layer=layer, first=first),
        grid=grid, in_specs=in_specs, out_specs=out_specs, out_shape=out_shape,
        scratch_shapes=scratch, name=f"prompt_layer{layer}", input_output_aliases=aliases,
        compiler_params=pltpu.CompilerParams(
            dimension_semantics=("arbitrary", "arbitrary"), vmem_limit_bytes=VMEM_LIMIT),
    )(*operands)


def _sample_layer(x, spool, sconv, ck, cv, layer, carried, gpre, gpost, win, poolw, pscale, convw, convb,
                  lng, lnb, wbr, wout):
    nb, nt, D = x.shape
    W = BRANCH_W
    nseq = SAMPLE_SEQS
    T = nseq * nt
    n_slabs = W // LANES
    small = lambda width: _whole_spec((DEPTH, width))
    state_in = lambda rows: pl.BlockSpec((None, rows, nseq, W), lambda g: (layer, 0, g, 0))
    cache = pl.BlockSpec((None, nseq, N_MEM * N_XHEADS, XHEAD_DIM), lambda g: (layer, g, 0, 0))
    in_specs = [
        pl.BlockSpec((nseq, nt, D), lambda g: (g, 0, 0)),
        state_in(POOL_BUF), state_in(CONV_BUF), cache, pl.BlockSpec(memory_space=pl.ANY),
        small(D), small(D),
        _whole_spec(win.shape),
        _layer_spec(poolw.shape[1:], layer),
        small(W),
        _layer_spec((CONV_WIDTH, W), layer),
        small(W), small(W), small(W),
        _whole_spec(wbr.shape),
        _whole_spec(wout.shape),
    ]
    first = carried is None
    out_specs = [pl.BlockSpec((nseq, nt, D), lambda g: (g, 0, 0))] + [
        _per_layer_spec((rows, nseq, W), lambda g: (0, g, 0), layer, first) for rows in (POOL_BUF, CONV_BUF)]
    out_shape = [jax.ShapeDtypeStruct((nb, nt, D), F32)] + [
        jax.ShapeDtypeStruct((DEPTH, rows, nb, W), F32) for rows in (POOL_BUF, CONV_BUF)]
    operands = [x, spool, sconv, ck, cv, gpre, gpost, win, poolw, pscale, convw, convb, lng, lnb, wbr, wout]
    aliases = _carry_layers(operands, in_specs, carried, first_output=1)
    scratch = [
        pltpu.VMEM((n_slabs, (POOL_BUF + nt) * nseq, LANES), F32),
        pltpu.VMEM((n_slabs, (CONV_BUF + nt) * nseq, LANES), F32),
        pltpu.VMEM((N_XHEADS, T, XHEAD_DIM), F32),
        pltpu.VMEM((N_XHEADS, T, XHEAD_DIM), F32),
        pltpu.VMEM((D // LANES, T, LANES), F32),
        pltpu.VMEM((nseq, N_MEM * N_XHEADS, XHEAD_DIM), F32),
        pltpu.SemaphoreType.DMA(()),
    ]
    body = _skip_carried(_sample_kernel, len(operands) - len(aliases), len(aliases))
    return pl.pallas_call(
        functools.partial(body, nseq=nseq, layer=layer, first=first),
        grid=(nb // nseq,), in_specs=in_specs, out_specs=out_specs, out_shape=out_shape,
        scratch_shapes=scratch, name=f"sample_layer{layer}", input_output_aliases=aliases,
        compiler_params=pltpu.CompilerParams(
            dimension_semantics=("arbitrary",), vmem_limit_bytes=VMEM_LIMIT),
    )(*operands)


def kernel(x_prompt, x_sample, state_pool, state_conv, cache_mem_k, cache_mem_v, mem_prompt,
           norm_pre, norm_post, mem_norm, w_mem_kv, w_in, pool_w, pool_scale, conv_w, conv_b,
           conv_ln_g, conv_ln_b, w_branch, w_out):
    W = BRANCH_W
    nb, nt, D = x_sample.shape
    gpre, gpost, gmem = norm_pre, norm_post, mem_norm
    pscale, convb, lng, lnb = pool_scale, conv_b, conv_ln_g, conv_ln_b
    ck = cache_mem_k.reshape(DEPTH, nb, N_MEM * N_XHEADS, XHEAD_DIM)
    cv = cache_mem_v.reshape(DEPTH, nb, N_MEM * N_XHEADS, XHEAD_DIM)

    spool_t = state_pool.transpose(0, 2, 1, 3)
    sconv_t = state_conv.transpose(0, 2, 1, 3)
    xs = x_sample

    xp = x_prompt
    prompt_new, sample_new = None, None
    for i in range(DEPTH):
        xp, *prompt_new = _prompt_layer(
            xp, mem_prompt, i, prompt_new, gpre, gpost, gmem, w_mem_kv, w_in, pool_w, pscale, conv_w,
            convb, lng, lnb, w_branch, w_out)
        win, wbr, wout = prompt_new[-3:]
        prompt_new = prompt_new[:-3]
        xs, *sample_new = _sample_layer(
            xs, spool_t, sconv_t, ck, cv, i, sample_new, gpre, gpost, win, pool_w, pscale, conv_w,
            convb, lng, lnb, wbr, wout)
    pool_p, conv_p, mk_p, mv_p = prompt_new
    pool_s, conv_s = (st.transpose(0, 2, 1, 3) for st in sample_new)
    heads = lambda kv: kv.reshape(DEPTH, kv.shape[1], N_MEM, N_XHEADS, XHEAD_DIM)
    return (xp, xs, pool_p, conv_p, heads(mk_p), heads(mv_p), pool_s, conv_s)
```

```python
import functools
import math

import jax
import jax.numpy as jnp
from jax import lax
from jax.experimental import pallas as pl
from jax.experimental.pallas import tpu as pltpu

D_MODEL = 1024
DEPTH = 2
DEC_SEQ = 8
PAST_LEN = 16384
BRANCH_W = D_MODEL // 2
N_BRANCH = 3
POOL_WINDOWS = (2, 4, 8, 16)
POOL_GROUP_W = BRANCH_W // len(POOL_WINDOWS)
POOL_BUF = max(POOL_WINDOWS) - 1
CONV_WIDTH = 31
CONV_BUF = CONV_WIDTH - 1
N_MEM = 256
N_XHEADS = 4
XHEAD_DIM = BRANCH_W // N_XHEADS
N_IN_SLICES = 7
EPS = 1e-6

LANES = 128
SUBLANES = 8
POOL_PAD = 8
POOL_HEAD = POOL_PAD + 16
CONV_HEAD = 32
PROMPT_TILE = 256
SAMPLE_SEQS = 16
STAGE_SLOTS = 4
WIDE_STAGE_ROWS = 64
NARROW_STAGE_ROWS = 256
VMEM_LIMIT = 60 * 1024 * 1024
NEG_LOG2E = -1.0 / math.log(2.0)
MASKED_SCORE = float("-inf")

F32 = jnp.float32
BF16 = jnp.bfloat16


def _sigmoid(x):
    return 1.0 / (1.0 + jnp.exp2(x * NEG_LOG2E))


def _silu(x):
    return x * _sigmoid(x)


def _rmsnorm(x, g):
    return x * lax.rsqrt(jnp.mean(x * x, axis=-1, keepdims=True) + EPS) * g


def _dot(a, b):
    return jnp.dot(a, b, preferred_element_type=F32)


def _pool_finish(acc_parts, p_in, p_gate, cnt_parts, poolw_ref, pscale_ref):
    ys = []
    for g in range(len(POOL_WINDOWS)):
        c0 = g * POOL_GROUP_W
        mixed = acc_parts[g] / cnt_parts[g] - p_in[:, c0:c0 + POOL_GROUP_W]
        ys.append(_dot(mixed.astype(BF16), poolw_ref[g].astype(BF16)))
    y = jnp.concatenate(ys, axis=-1)
    return y * pscale_ref[...] * _silu(p_gate)


def _conv_finish(cv, c_gate, convb_ref, lng_ref, lnb_ref):
    cv = cv + convb_ref[...]
    mu = jnp.mean(cv, axis=-1, keepdims=True)
    d = cv - mu
    var = jnp.mean(d * d, axis=-1, keepdims=True)
    r = d * lax.rsqrt(var + EPS) * lng_ref[...] + lnb_ref[...]
    return _silu(r) * _silu(c_gate)


def _merge_out(x, h, branches, win_ref, wbr_ref, wout_ref, gpost_ref):
    merged = None
    base = N_IN_SLICES * BRANCH_W
    for n, br in enumerate(branches):
        proj = _dot(br.astype(BF16), wbr_ref[n])
        gate = _sigmoid(_dot(h, win_ref[:, base + n * D_MODEL:base + (n + 1) * D_MODEL]))
        term = gate * proj
        merged = term if merged is None else merged + term
    y = _dot(merged.astype(BF16), wout_ref[...])
    return x + _rmsnorm(y, gpost_ref[...])


def _layer_row(ref, layer):
    return ref.at[pl.ds(layer, 1)]


def _own_layer(ref, layer, first):
    return ref.at[layer] if first else ref


def _zero_other_layers(ref, layer):
    for other in range(ref.shape[0]):
        if other != layer:
            ref[other] = jnp.zeros(ref.shape[1:], ref.dtype)


def _stage_and_cast(streams):
    copies = [[pltpu.make_async_copy(src, stage.at[i % stage.shape[0]], sem.at[i % stage.shape[0]])
               for i, (src, _) in enumerate(chunks)] for chunks, stage, sem in streams]
    for stream, (_, stage, _) in zip(copies, streams):
        for copy in stream[:stage.shape[0] - 1]:
            copy.start()
    for i in range(max(len(stream) for stream in copies)):
        for stream, (_, stage, _) in zip(copies, streams):
            ahead = i + stage.shape[0] - 1
            if ahead < len(stream):
                stream[ahead].start()
        for stream, (chunks, stage, _) in zip(copies, streams):
            if i < len(stream):
                stream[i].wait()
                chunks[i][1][...] = stage[i % stage.shape[0]].astype(BF16)


def _prompt_kernel(x_ref, mem_ref, gpre_ref, gpost_ref, gmem_ref, wkv_hbm, win_hbm, poolw_ref,
                   pscale_ref, convw_ref, convb_ref, lng_ref, lnb_ref, wbr_hbm, wout_hbm,
                   y_ref, npool_ref, nconv_ref, mk_ref, mv_ref, win_out, wbr_out, wout_out,
                   pool_ext, pool_tmp, conv_ext, conv_sh, gate_s, kt_s, v_s,
                   wkv_ref, win_ref, wbr_ref, wout_ref, stage_wide, stage_narrow, sem_wide, sem_narrow,
                   sem_out, *, ts, layer, first):
    W = BRANCH_W
    s = pl.program_id(1)
    first_step = jnp.logical_and(pl.program_id(0) == 0, s == 0)
    gpre_ref, gpost_ref, gmem_ref, pscale_ref, convb_ref, lng_ref, lnb_ref = (
        _layer_row(r, layer) for r in (gpre_ref, gpost_ref, gmem_ref, pscale_ref, convb_ref, lng_ref, lnb_ref))
    per_layer = (npool_ref, nconv_ref, mk_ref, mv_ref)
    npool_ref, nconv_ref, mk_ref, mv_ref = (_own_layer(r, layer, first) for r in per_layer)

    def hand_over():
        return [pltpu.make_async_copy(src, dst, sem_out.at[i]) for i, (src, dst) in enumerate(
            ((win_ref, win_out), (wbr_ref, wbr_out), (wout_ref, wout_out)))]

    @pl.when(first_step)
    def _():
        nrows = stage_narrow.shape[1]
        narrow = [(wkv_hbm.at[layer, pl.ds(r, nrows), :], wkv_ref.at[pl.ds(r, nrows), :])
                  for r in range(0, D_MODEL, nrows)]
        narrow += [(wbr_hbm.at[layer, n, pl.ds(r, nrows), :], wbr_ref.at[n, pl.ds(r, nrows), :])
                   for n in range(N_BRANCH) for r in range(0, W, nrows)]
        narrow += [(wout_hbm.at[layer, pl.ds(r, nrows), :], wout_ref.at[pl.ds(r, nrows), :])
                   for r in range(0, D_MODEL, nrows)]
        wrows = stage_wide.shape[1]
        wide = [(win_hbm.at[layer, pl.ds(r, wrows), :], win_ref.at[pl.ds(r, wrows), :])
                for r in range(0, D_MODEL, wrows)]
        _stage_and_cast([(narrow, stage_narrow, sem_narrow), (wide, stage_wide, sem_wide)])
        for copy in hand_over():
            copy.start()

    @pl.when(s == 0)
    def _():
        if first:
            for r in per_layer:
                _zero_other_layers(r, layer)
        pool_ext[0:POOL_HEAD, :] = jnp.zeros((POOL_HEAD, W), F32)
        pool_tmp[:, 0:POOL_PAD, :] = jnp.zeros((pool_tmp.shape[0], POOL_PAD, POOL_GROUP_W), F32)
        conv_ext[0:CONV_HEAD, :] = jnp.zeros((CONV_HEAD, W), F32)
        hm = _rmsnorm(mem_ref[0], gmem_ref[...]).astype(BF16)
        kv = _dot(hm, wkv_ref[...])
        k = kv[:, :W]
        v = kv[:, W:]
        for hd in range(N_XHEADS):
            rows = pl.ds(hd, N_MEM, stride=N_XHEADS)
            mk_ref[0, rows, :] = k[:, hd * XHEAD_DIM:(hd + 1) * XHEAD_DIM]
            mv_ref[0, rows, :] = v[:, hd * XHEAD_DIM:(hd + 1) * XHEAD_DIM]
        kt_s[...] = k.T.astype(BF16)
        v_s[...] = v.astype(BF16)

    x = x_ref[0]
    h = _rmsnorm(x, gpre_ref[...]).astype(BF16)

    pb = _dot(h, win_ref[:, 2 * W:5 * W])
    u = pb[:, :W] * _sigmoid(pb[:, W:2 * W])
    c_gate = pb[:, 2 * W:]
    conv_ext[CONV_HEAD:CONV_HEAD + ts, :] = u
    for r in range(1, SUBLANES):
        conv_sh[r - 1, :, :] = conv_ext[r:r + ts + CONV_HEAD - SUBLANES, :]

    off = CONV_HEAD - CONV_BUF
    gate_base = N_IN_SLICES * W
    n_slabs = W // LANES
    gate_chunk = N_BRANCH * D_MODEL // n_slabs
    slabs = []
    for c in range(n_slabs):
        g0 = c * gate_chunk
        gate_s[:, g0:g0 + gate_chunk] = _sigmoid(
            _dot(h, win_ref[:, gate_base + g0:gate_base + g0 + gate_chunk]))
        c0 = c * LANES
        acc = None
        for k in range(CONV_WIDTH):
            a8, r = divmod(off + k, SUBLANES)
            if r == 0:
                rows = conv_ext[a8 * SUBLANES:a8 * SUBLANES + ts, c0:c0 + LANES]
            else:
                rows = conv_sh[r - 1, a8 * SUBLANES:a8 * SUBLANES + ts, c0:c0 + LANES]
            term = rows * convw_ref[k:k + 1, c0:c0 + LANES]
            acc = term if acc is None else acc + term
        slabs.append(acc)
    bconv = _conv_finish(jnp.concatenate(slabs, axis=-1), c_gate, convb_ref, lng_ref, lnb_ref)
    nconv_ref[0] = conv_ext[CONV_HEAD + ts - CONV_BUF:CONV_HEAD + ts, :]
    conv_ext[0:CONV_HEAD, :] = conv_ext[ts:ts + CONV_HEAD, :]

    pa = _dot(h, win_ref[:, 0:2 * W])
    p_in = pa[:, :W]
    p_gate = pa[:, W:]
    pool_ext[POOL_HEAD:POOL_HEAD + ts, :] = p_in
    pos1 = (lax.broadcasted_iota(jnp.int32, (ts, POOL_GROUP_W), 0) + s * ts + 1).astype(F32)
    accs, cnts = [], []
    for g, win in enumerate(POOL_WINDOWS):
        c0 = g * POOL_GROUP_W
        level, shift = None, 1
        while shift < win:
            last = 2 * shift == win
            lo = POOL_HEAD if last else SUBLANES
            n = POOL_HEAD + ts - lo
            if level is None:
                cur = (pool_ext[lo:lo + n, c0:c0 + POOL_GROUP_W]
                       + pool_ext[lo - shift:lo - shift + n, c0:c0 + POOL_GROUP_W])
            else:
                cur = pool_tmp[level, lo:lo + n, :] + pool_tmp[level, lo - shift:lo - shift + n, :]
            if last:
                accs.append(cur)
            else:
                level = 0 if level is None else level + 1
                pool_tmp[level, lo:lo + n, :] = cur
            shift *= 2
        cnts.append(jnp.minimum(pos1, float(win)))
    a = _pool_finish(accs, p_in, p_gate, cnts, poolw_ref, pscale_ref)
    npool_ref[0] = pool_ext[POOL_HEAD + ts - POOL_BUF:POOL_HEAD + ts, :]
    pool_ext[POOL_PAD:POOL_HEAD, :] = pool_ext[ts + POOL_PAD:ts + POOL_HEAD, :]

    pc = _dot(h, win_ref[:, 5 * W:7 * W])
    q = pc[:, :W]
    x_gate = pc[:, W:]
    scale = 1.0 / math.sqrt(XHEAD_DIM)
    outs = []
    for hd in range(N_XHEADS):
        c0 = hd * XHEAD_DIM
        sc = _dot((q[:, c0:c0 + XHEAD_DIM] * scale).astype(BF16), kt_s[c0:c0 + XHEAD_DIM, :])
        e = jnp.exp(sc - jnp.max(sc, axis=-1, keepdims=True))
        l = jnp.sum(e, axis=-1, keepdims=True)
        outs.append(_dot(e.astype(BF16), v_s[:, c0:c0 + XHEAD_DIM]) / l)
    cattn = jnp.concatenate(outs, axis=-1) * _silu(x_gate)

    merged = None
    for n, br in enumerate((a, bconv, cattn)):
        term = gate_s[:, n * D_MODEL:(n + 1) * D_MODEL] * _dot(br.astype(BF16), wbr_ref[n])
        merged = term if merged is None else merged + term
    y = _dot(merged.astype(BF16), wout_ref[...])
    y_ref[0] = x + _rmsnorm(y, gpost_ref[...])

    @pl.when(jnp.logical_and(pl.program_id(0) == pl.num_programs(0) - 1, s == pl.num_programs(1) - 1))
    def _():
        for copy in hand_over():
            copy.wait()


def _sample_kernel(x_ref, spool_ref, sconv_ref, k_ref, v_ref, gpre_ref, gpost_ref, win_ref,
                   poolw_ref, pscale_ref, convw_ref, convb_ref, lng_ref, lnb_ref, wbr_ref, wout_ref,
                   y_ref, npool_ref, nconv_ref, pool_ext, conv_ext, q_s, o_s, xy_s, v_buf, v_sem,
                   *, nseq, layer, first):
    W = BRANCH_W
    T = nseq * DEC_SEQ
    n_slabs = W // LANES
    step = pl.program_id(0)
    gpre_ref, gpost_ref, pscale_ref, convb_ref, lng_ref, lnb_ref = (
        _layer_row(r, layer) for r in (gpre_ref, gpost_ref, pscale_ref, convb_ref, lng_ref, lnb_ref))
    if first:
        _zero_other_layers(npool_ref, layer)
        _zero_other_layers(nconv_ref, layer)
    npool_ref = _own_layer(npool_ref, layer, first)
    nconv_ref = _own_layer(nconv_ref, layer, first)

    def v_copy(block):
        return pltpu.make_async_copy(v_ref.at[layer, pl.ds(block * nseq, nseq)], v_buf, v_sem)

    @pl.when(step == 0)
    def _():
        v_copy(0).start()

    x_seq = x_ref[...].reshape(T, D_MODEL)
    for c in range(D_MODEL // LANES):
        xy_s[c] = x_seq[:, c * LANES:(c + 1) * LANES]
    x = jnp.concatenate([
        jnp.concatenate([xy_s[c, pl.ds(t, nseq, stride=DEC_SEQ), :] for c in range(D_MODEL // LANES)], axis=-1)
        for t in range(DEC_SEQ)], axis=0)
    h = _rmsnorm(x, gpre_ref[...]).astype(BF16)

    pa = _dot(h, win_ref[:, 0:2 * W])
    p_in = pa[:, :W]
    p_gate = pa[:, W:]
    pos1 = (lax.broadcasted_iota(jnp.int32, (DEC_SEQ, nseq, POOL_GROUP_W), 0) + PAST_LEN + 1).astype(F32)
    pos1 = pos1.reshape(T, POOL_GROUP_W)
    accs, cnts = [], []
    for g, win in enumerate(POOL_WINDOWS):
        c0 = g * POOL_GROUP_W
        pool_ext[g, 0:POOL_BUF * nseq, :] = spool_ref[:, :, c0:c0 + POOL_GROUP_W].reshape(
            POOL_BUF * nseq, POOL_GROUP_W)
        pool_ext[g, POOL_BUF * nseq:(POOL_BUF + DEC_SEQ) * nseq, :] = p_in[:, c0:c0 + POOL_GROUP_W]
        acc = None
        for i in range(win):
            r0 = (POOL_BUF - i) * nseq
            rows = pool_ext[g, r0:r0 + T, :]
            acc = rows if acc is None else acc + rows
        accs.append(acc)
        cnts.append(jnp.minimum(pos1, float(win)))
    a = _pool_finish(accs, p_in, p_gate, cnts, poolw_ref, pscale_ref)
    for g in range(n_slabs):
        npool_ref[:, :, g * LANES:(g + 1) * LANES] = pool_ext[
            g, DEC_SEQ * nseq:(DEC_SEQ + POOL_BUF) * nseq, :].reshape(POOL_BUF, nseq, LANES)

    pb = _dot(h, win_ref[:, 2 * W:5 * W])
    u = pb[:, :W] * _sigmoid(pb[:, W:2 * W])
    c_gate = pb[:, 2 * W:]
    slabs = []
    for c in range(n_slabs):
        c0 = c * LANES
        conv_ext[c, 0:CONV_BUF * nseq, :] = sconv_ref[:, :, c0:c0 + LANES].reshape(CONV_BUF * nseq, LANES)
        conv_ext[c, CONV_BUF * nseq:(CONV_BUF + DEC_SEQ) * nseq, :] = u[:, c0:c0 + LANES]
        acc = None
        for k in range(CONV_WIDTH):
            term = conv_ext[c, k * nseq:k * nseq + T, :] * convw_ref[k:k + 1, c0:c0 + LANES]
            acc = term if acc is None else acc + term
        slabs.append(acc)
    bconv = _conv_finish(jnp.concatenate(slabs, axis=-1), c_gate, convb_ref, lng_ref, lnb_ref)
    for c in range(n_slabs):
        nconv_ref[:, :, c * LANES:(c + 1) * LANES] = conv_ext[
            c, DEC_SEQ * nseq:(DEC_SEQ + CONV_BUF) * nseq, :].reshape(CONV_BUF, nseq, LANES)

    pc = _dot(h, win_ref[:, 5 * W:7 * W])
    x_gate = pc[:, W:]
    for hd in range(N_XHEADS):
        q_s[hd] = pc[:, hd * XHEAD_DIM:(hd + 1) * XHEAD_DIM]
    scale = 1.0 / math.sqrt(XHEAD_DIM)
    nq = N_XHEADS * DEC_SEQ
    nk = N_MEM * N_XHEADS
    q4 = jnp.stack([
        jnp.concatenate([q_s[hd, pl.ds(sq, DEC_SEQ, stride=nseq), :] for hd in range(N_XHEADS)], axis=0)
        for sq in range(nseq)]).astype(BF16)
    sc = jnp.einsum('gqd,gkd->gqk', q4, k_ref[...].astype(BF16), preferred_element_type=F32) * scale
    q_head = lax.broadcasted_iota(jnp.int32, (nq, nk), 0) // DEC_SEQ
    k_head = lax.broadcasted_iota(jnp.int32, (nq, nk), 1) % N_XHEADS
    sc = jnp.where((q_head == k_head)[None], sc, MASKED_SCORE)
    e = jnp.exp(sc - jnp.max(sc, axis=-1, keepdims=True))
    l = jnp.sum(e, axis=-1, keepdims=True)
    v_copy(step).wait()
    o = jnp.einsum('gqk,gkd->gqd', e.astype(BF16), v_buf[...].astype(BF16),
                   preferred_element_type=F32) / l
    for sq in range(nseq):
        for hd in range(N_XHEADS):
            o_s[hd, pl.ds(sq, DEC_SEQ, stride=nseq), :] = o[sq, hd * DEC_SEQ:(hd + 1) * DEC_SEQ, :]
    cattn = jnp.concatenate([o_s[hd] for hd in range(N_XHEADS)], axis=-1) * _silu(x_gate)

    y = _merge_out(x, h, (a, bconv, cattn), win_ref, wbr_ref, wout_ref, gpost_ref)
    for c in range(D_MODEL // LANES):
        xy_s[c] = y[:, c * LANES:(c + 1) * LANES]
    for sq in range(nseq):
        y_ref[sq] = jnp.concatenate(
            [xy_s[c, pl.ds(sq, DEC_SEQ, stride=nseq), :] for c in range(D_MODEL // LANES)], axis=-1)

    @pl.when(step + 1 < pl.num_programs(0))
    def _():
        v_copy(step + 1).start()


def _layer_spec(shape, layer):
    zeros = (0,) * len(shape)
    return pl.BlockSpec((None,) + shape, lambda *_: (layer,) + zeros, pipeline_mode=pl.Buffered(1))


def _whole_spec(shape):
    zeros = (0,) * len(shape)
    return pl.BlockSpec(shape, lambda *_: zeros, pipeline_mode=pl.Buffered(1))


def _carry_layers(operands, in_specs, carried, first_output):
    aliases = {}
    for i, arr in enumerate(carried or ()):
        aliases[len(operands)] = first_output + i
        operands.append(arr)
        in_specs.append(pl.BlockSpec(memory_space=pl.ANY))
    return aliases


def _per_layer_spec(block, index_map, layer, first):
    if first:
        return pl.BlockSpec((DEPTH,) + block, lambda *i: (0,) + index_map(*i))
    return pl.BlockSpec((None,) + block, lambda *i: (layer,) + index_map(*i))


def _skip_carried(body, n_inputs, n_carried):
    def wrapped(*refs, **kw):
        return body(*refs[:n_inputs], *refs[n_inputs + n_carried:], **kw)
    return wrapped


def _prompt_layer(x, mem, layer, carried, gpre, gpost, gmem, wkv, win, poolw, pscale, convw, convb,
                  lng, lnb, wbr, wout):
    B, S, D = x.shape
    W = BRANCH_W
    ts = PROMPT_TILE
    grid = (B, S // ts)
    small = lambda width: _whole_spec((DEPTH, width))
    in_hbm = pl.BlockSpec(memory_space=pl.ANY)
    in_specs = [
        pl.BlockSpec((1, ts, D), lambda b, s: (b, s, 0)),
        pl.BlockSpec((1, N_MEM, D), lambda b, s: (b, 0, 0)),
        small(D), small(D), small(D),
        in_hbm, in_hbm,
        _layer_spec(poolw.shape[1:], layer),
        small(W),
        _layer_spec((CONV_WIDTH, W), layer),
        small(W), small(W), small(W),
        in_hbm, in_hbm,
    ]
    per_layer = [(POOL_BUF, W), (CONV_BUF, W), (N_MEM * N_XHEADS, XHEAD_DIM), (N_MEM * N_XHEADS, XHEAD_DIM)]
    first = carried is None
    out_specs = [pl.BlockSpec((1, ts, D), lambda b, s: (b, s, 0))] + [
        _per_layer_spec((1,) + shape, lambda b, s: (b, 0, 0), layer, first) for shape in per_layer]
    out_shape = [jax.ShapeDtypeStruct((B, S, D), F32)] + [
        jax.ShapeDtypeStruct((DEPTH, B) + shape, F32) for shape in per_layer]
    handed = [win.shape[1:], wbr.shape[1:], wout.shape[1:]]
    out_specs += [pl.BlockSpec(memory_space=pl.ANY)] * len(handed)
    out_shape += [jax.ShapeDtypeStruct(shape, BF16) for shape in handed]
    operands = [x, mem, gpre, gpost, gmem, wkv, win, poolw, pscale, convw, convb, lng, lnb, wbr, wout]
    aliases = _carry_layers(operands, in_specs, carried, first_output=1)
    scratch = [
        pltpu.VMEM((POOL_HEAD + ts, W), F32),
        pltpu.VMEM((len(POOL_WINDOWS) - 1, POOL_HEAD + ts, POOL_GROUP_W), F32),
        pltpu.VMEM((CONV_HEAD + ts, W), F32),
        pltpu.VMEM((SUBLANES - 1, ts + CONV_HEAD - SUBLANES, W), F32),
        pltpu.VMEM((ts, N_BRANCH * D_MODEL), F32),
        pltpu.VMEM((W, N_MEM), BF16),
        pltpu.VMEM((N_MEM, W), BF16),
        pltpu.VMEM(wkv.shape[1:], BF16),
        pltpu.VMEM(win.shape[1:], BF16),
        pltpu.VMEM(wbr.shape[1:], BF16),
        pltpu.VMEM(wout.shape[1:], BF16),
        pltpu.VMEM((STAGE_SLOTS, WIDE_STAGE_ROWS, win.shape[-1]), F32),
        pltpu.VMEM((STAGE_SLOTS, NARROW_STAGE_ROWS, D), F32),
        pltpu.SemaphoreType.DMA((STAGE_SLOTS,)),
        pltpu.SemaphoreType.DMA((STAGE_SLOTS,)),
        pltpu.SemaphoreType.DMA((len(handed),)),
    ]
    body = _skip_carried(_prompt_kernel, len(operands) - len(aliases), len(aliases))
    return pl.pallas_call(
        functools.partial(body, ts=ts, layer=layer, first=first),
        grid=grid, in_specs=in_specs, out_specs=out_specs, out_shape=out_shape,
        scratch_shapes=scratch, name=f"prompt_layer{layer}", input_output_aliases=aliases,
        compiler_params=pltpu.CompilerParams(
            dimension_semantics=("arbitrary", "arbitrary"), vmem_limit_bytes=VMEM_LIMIT),
    )(*operands)


def _sample_layer(x, spool, sconv, ck, cv, layer, carried, gpre, gpost, win, poolw, pscale, convw, convb,
                  lng, lnb, wbr, wout):
    nb, nt, D = x.shape
    W = BRANCH_W
    nseq = SAMPLE_SEQS
    T = nseq * nt
    n_slabs = W // LANES
    small = lambda width: _whole_spec((DEPTH, width))
    state_in = lambda rows: pl.BlockSpec((None, rows, nseq, W), lambda g: (layer, 0, g, 0))
    cache = pl.BlockSpec((None, nseq, N_MEM * N_XHEADS, XHEAD_DIM), lambda g: (layer, g, 0, 0))
    in_specs = [
        pl.BlockSpec((nseq, nt, D), lambda g: (g, 0, 0)),
        state_in(POOL_BUF), state_in(CONV_BUF), cache, pl.BlockSpec(memory_space=pl.ANY),
        small(D), small(D),
        _whole_spec(win.shape),
        _layer_spec(poolw.shape[1:], layer),
        small(W),
        _layer_spec((CONV_WIDTH, W), layer),
        small(W), small(W), small(W),
        _whole_spec(wbr.shape),
        _whole_spec(wout.shape),
    ]
    first = carried is None
    out_specs = [pl.BlockSpec((nseq, nt, D), lambda g: (g, 0, 0))] + [
        _per_layer_spec((rows, nseq, W), lambda g: (0, g, 0), layer, first) for rows in (POOL_BUF, CONV_BUF)]
    out_shape = [jax.ShapeDtypeStruct((nb, nt, D), F32)] + [
        jax.ShapeDtypeStruct((DEPTH, rows, nb, W), F32) for rows in (POOL_BUF, CONV_BUF)]
    operands = [x, spool, sconv, ck, cv, gpre, gpost, win, poolw, pscale, convw, convb, lng, lnb, wbr, wout]
    aliases = _carry_layers(operands, in_specs, carried, first_output=1)
    scratch = [
        pltpu.VMEM((n_slabs, (POOL_BUF + nt) * nseq, LANES), F32),
        pltpu.VMEM((n_slabs, (CONV_BUF + nt) * nseq, LANES), F32),
        pltpu.VMEM((N_XHEADS, T, XHEAD_DIM), F32),
        pltpu.VMEM((N_XHEADS, T, XHEAD_DIM), F32),
        pltpu.VMEM((D // LANES, T, LANES), F32),
        pltpu.VMEM((nseq, N_MEM * N_XHEADS, XHEAD_DIM), F32),
        pltpu.SemaphoreType.DMA(()),
    ]
    body = _skip_carried(_sample_kernel, len(operands) - len(aliases), len(aliases))
    return pl.pallas_call(
        functools.partial(body, nseq=nseq, layer=layer, first=first),
        grid=(nb // nseq,), in_specs=in_specs, out_specs=out_specs, out_shape=out_shape,
        scratch_shapes=scratch, name=f"sample_layer{layer}", input_output_aliases=aliases,
        compiler_params=pltpu.CompilerParams(
            dimension_semantics=("arbitrary",), vmem_limit_bytes=VMEM_LIMIT),
    )(*operands)


def kernel(x_prompt, x_sample, state_pool, state_conv, cache_mem_k, cache_mem_v, mem_prompt,
           norm_pre, norm_post, mem_norm, w_mem_kv, w_in, pool_w, pool_scale, conv_w, conv_b,
           conv_ln_g, conv_ln_b, w_branch, w_out):
    W = BRANCH_W
    nb, nt, D = x_sample.shape
    gpre, gpost, gmem = norm_pre, norm_post, mem_norm
    pscale, convb, lng, lnb = pool_scale, conv_b, conv_ln_g, conv_ln_b
    ck = cache_mem_k.reshape(DEPTH, nb, N_MEM * N_XHEADS, XHEAD_DIM)
    cv = cache_mem_v.reshape(DEPTH, nb, N_MEM * N_XHEADS, XHEAD_DIM)

    spool_t = state_pool.transpose(0, 2, 1, 3)
    sconv_t = state_conv.transpose(0, 2, 1, 3)
    xs = x_sample

    xp = x_prompt
    prompt_new, sample_new = None, None
    for i in range(DEPTH):
        xp, *prompt_new = _prompt_layer(
            xp, mem_prompt, i, prompt_new, gpre, gpost, gmem, w_mem_kv, w_in, pool_w, pscale, conv_w,
            convb, lng, lnb, w_branch, w_out)
        win, wbr, wout = prompt_new[-3:]
        prompt_new = prompt_new[:-3]
        xs, *sample_new = _sample_layer(
            xs, spool_t, sconv_t, ck, cv, i, sample_new, gpre, gpost, win, pool_w, pscale, conv_w,
            convb, lng, lnb, wbr, wout)
    pool_p, conv_p, mk_p, mv_p = prompt_new
    pool_s, conv_s = (st.transpose(0, 2, 1, 3) for st in sample_new)
    heads = lambda kv: kv.reshape(DEPTH, kv.shape[1], N_MEM, N_XHEADS, XHEAD_DIM)
    return (xp, xs, pool_p, conv_p, heads(mk_p), heads(mv_p), pool_s, conv_s)
```

```python
import functools
import math

import jax
import jax.numpy as jnp
from jax import lax
from jax.experimental import pallas as pl
from jax.experimental.pallas import tpu as pltpu

D_MODEL = 1024
DEPTH = 2
DEC_SEQ = 8
PAST_LEN = 16384
BRANCH_W = D_MODEL // 2
N_BRANCH = 3
POOL_WINDOWS = (2, 4, 8, 16)
POOL_GROUP_W = BRANCH_W // len(POOL_WINDOWS)
POOL_BUF = max(POOL_WINDOWS) - 1
CONV_WIDTH = 31
CONV_BUF = CONV_WIDTH - 1
N_MEM = 256
N_XHEADS = 4
XHEAD_DIM = BRANCH_W // N_XHEADS
N_IN_SLICES = 7
EPS = 1e-6

LANES = 128
SUBLANES = 8
POOL_PAD = 8
POOL_HEAD = POOL_PAD + 16
CONV_HEAD = 32
PROMPT_TILE = 256
SAMPLE_SEQS = 16
STAGE_SLOTS = 4
WIDE_STAGE_ROWS = 64
NARROW_STAGE_ROWS = 256
VMEM_LIMIT = 60 * 1024 * 1024
NEG_LOG2E = -1.0 / math.log(2.0)
MASKED_SCORE = float("-inf")

F32 = jnp.float32
BF16 = jnp.bfloat16


def _sigmoid(x):
    return 1.0 / (1.0 + jnp.exp2(x * NEG_LOG2E))


def _silu(x):
    return x * _sigmoid(x)


def _rmsnorm(x, g):
    return x * lax.rsqrt(jnp.mean(x * x, axis=-1, keepdims=True) + EPS) * g


def _dot(a, b):
    return jnp.dot(a, b, preferred_element_type=F32)


def _pool_finish(acc_parts, p_in, p_gate, cnt_parts, poolw_ref, pscale_ref):
    ys = []
    for g in range(len(POOL_WINDOWS)):
        c0 = g * POOL_GROUP_W
        mixed = acc_parts[g] / cnt_parts[g] - p_in[:, c0:c0 + POOL_GROUP_W]
        ys.append(_dot(mixed.astype(BF16), poolw_ref[g].astype(BF16)))
    y = jnp.concatenate(ys, axis=-1)
    return y * pscale_ref[...] * _silu(p_gate)


def _conv_finish(cv, c_gate, convb_ref, lng_ref, lnb_ref):
    cv = cv + convb_ref[...]
    mu = jnp.mean(cv, axis=-1, keepdims=True)
    d = cv - mu
    var = jnp.mean(d * d, axis=-1, keepdims=True)
    r = d * lax.rsqrt(var + EPS) * lng_ref[...] + lnb_ref[...]
    return _silu(r) * _silu(c_gate)


def _merge_out(x, h, branches, win_ref, wbr_ref, wout_ref, gpost_ref):
    merged = None
    base = N_IN_SLICES * BRANCH_W
    for n, br in enumerate(branches):
        proj = _dot(br.astype(BF16), wbr_ref[n])
        gate = _sigmoid(_dot(h, win_ref[:, base + n * D_MODEL:base + (n + 1) * D_MODEL]))
        term = gate * proj
        merged = term if merged is None else merged + term
    y = _dot(merged.astype(BF16), wout_ref[...])
    return x + _rmsnorm(y, gpost_ref[...])


def _layer_row(ref, layer):
    return ref.at[pl.ds(layer, 1)]


def _own_layer(ref, layer, first):
    return ref.at[layer] if first else ref


def _zero_other_layers(ref, layer):
    for other in range(ref.shape[0]):
        if other != layer:
            ref[other] = jnp.zeros(ref.shape[1:], ref.dtype)


def _stage_and_cast(streams):
    copies = [[pltpu.make_async_copy(src, stage.at[i % stage.shape[0]], sem.at[i % stage.shape[0]])
               for i, (src, _) in enumerate(chunks)] for chunks, stage, sem in streams]
    for stream, (_, stage, _) in zip(copies, streams):
        for copy in stream[:stage.shape[0] - 1]:
            copy.start()
    for i in range(max(len(stream) for stream in copies)):
        for stream, (_, stage, _) in zip(copies, streams):
            ahead = i + stage.shape[0] - 1
            if ahead < len(stream):
                stream[ahead].start()
        for stream, (chunks, stage, _) in zip(copies, streams):
            if i < len(stream):
                stream[i].wait()
                chunks[i][1][...] = stage[i % stage.shape[0]].astype(BF16)


def _prompt_kernel(x_ref, mem_ref, gpre_ref, gpost_ref, gmem_ref, wkv_hbm, win_hbm, poolw_ref,
                   pscale_ref, convw_ref, convb_ref, lng_ref, lnb_ref, wbr_hbm, wout_hbm,
                   y_ref, npool_ref, nconv_ref, mk_ref, mv_ref, win_out, wbr_out, wout_out,
                   pool_ext, pool_tmp, conv_ext, conv_sh, gate_s, kt_s, v_s,
                   wkv_ref, win_ref, wbr_ref, wout_ref, stage_wide, stage_narrow, sem_wide, sem_narrow,
                   sem_out, *, ts, layer, first):
    W = BRANCH_W
    s = pl.program_id(1)
    first_step = jnp.logical_and(pl.program_id(0) == 0, s == 0)
    gpre_ref, gpost_ref, gmem_ref, pscale_ref, convb_ref, lng_ref, lnb_ref = (
        _layer_row(r, layer) for r in (gpre_ref, gpost_ref, gmem_ref, pscale_ref, convb_ref, lng_ref, lnb_ref))
    per_layer = (npool_ref, nconv_ref, mk_ref, mv_ref)
    npool_ref, nconv_ref, mk_ref, mv_ref = (_own_layer(r, layer, first) for r in per_layer)

    def hand_over():
        return [pltpu.make_async_copy(src, dst, sem_out.at[i]) for i, (src, dst) in enumerate(
            ((win_ref, win_out), (wbr_ref, wbr_out), (wout_ref, wout_out)))]

    @pl.when(first_step)
    def _():
        nrows = stage_narrow.shape[1]
        narrow = [(wkv_hbm.at[layer, pl.ds(r, nrows), :], wkv_ref.at[pl.ds(r, nrows), :])
                  for r in range(0, D_MODEL, nrows)]
        narrow += [(wbr_hbm.at[layer, n, pl.ds(r, nrows), :], wbr_ref.at[n, pl.ds(r, nrows), :])
                   for n in range(N_BRANCH) for r in range(0, W, nrows)]
        narrow += [(wout_hbm.at[layer, pl.ds(r, nrows), :], wout_ref.at[pl.ds(r, nrows), :])
                   for r in range(0, D_MODEL, nrows)]
        wrows = stage_wide.shape[1]
        wide = [(win_hbm.at[layer, pl.ds(r, wrows), :], win_ref.at[pl.ds(r, wrows), :])
                for r in range(0, D_MODEL, wrows)]
        _stage_and_cast([(narrow, stage_narrow, sem_narrow), (wide, stage_wide, sem_wide)])
        for copy in hand_over():
            copy.start()

    @pl.when(s == 0)
    def _():
        if first:
            for r in per_layer:
                _zero_other_layers(r, layer)
        pool_ext[0:POOL_HEAD, :] = jnp.zeros((POOL_HEAD, W), F32)
        pool_tmp[:, 0:POOL_PAD, :] = jnp.zeros((pool_tmp.shape[0], POOL_PAD, POOL_GROUP_W), F32)
        conv_ext[0:CONV_HEAD, :] = jnp.zeros((CONV_HEAD, W), F32)
        hm = _rmsnorm(mem_ref[0], gmem_ref[...]).astype(BF16)
        kv = _dot(hm, wkv_ref[...])
        k = kv[:, :W]
        v = kv[:, W:]
        for hd in range(N_XHEADS):
            rows = pl.ds(hd, N_MEM, stride=N_XHEADS)
            mk_ref[0, rows, :] = k[:, hd * XHEAD_DIM:(hd + 1) * XHEAD_DIM]
            mv_ref[0, rows, :] = v[:, hd * XHEAD_DIM:(hd + 1) * XHEAD_DIM]
        kt_s[...] = k.T.astype(BF16)
        v_s[...] = v.astype(BF16)

    x = x_ref[0]
    h = _rmsnorm(x, gpre_ref[...]).astype(BF16)

    pb = _dot(h, win_ref[:, 2 * W:5 * W])
    u = pb[:, :W] * _sigmoid(pb[:, W:2 * W])
    c_gate = pb[:, 2 * W:]
    conv_ext[CONV_HEAD:CONV_HEAD + ts, :] = u
    for r in range(1, SUBLANES):
        conv_sh[r - 1, :, :] = conv_ext[r:r + ts + CONV_HEAD - SUBLANES, :]

    off = CONV_HEAD - CONV_BUF
    gate_base = N_IN_SLICES * W
    n_slabs = W // LANES
    gate_chunk = N_BRANCH * D_MODEL // n_slabs
    slabs = []
    for c in range(n_slabs):
        g0 = c * gate_chunk
        gate_s[:, g0:g0 + gate_chunk] = _sigmoid(
            _dot(h, win_ref[:, gate_base + g0:gate_base + g0 + gate_chunk]))
        c0 = c * LANES
        acc = None
        for k in range(CONV_WIDTH):
            a8, r = divmod(off + k, SUBLANES)
            if r == 0:
                rows = conv_ext[a8 * SUBLANES:a8 * SUBLANES + ts, c0:c0 + LANES]
            else:
                rows = conv_sh[r - 1, a8 * SUBLANES:a8 * SUBLANES + ts, c0:c0 + LANES]
            term = rows * convw_ref[k:k + 1, c0:c0 + LANES]
            acc = term if acc is None else acc + term
        slabs.append(acc)
    bconv = _conv_finish(jnp.concatenate(slabs, axis=-1), c_gate, convb_ref, lng_ref, lnb_ref)
    nconv_ref[0] = conv_ext[CONV_HEAD + ts - CONV_BUF:CONV_HEAD + ts, :]
    conv_ext[0:CONV_HEAD, :] = conv_ext[ts:ts + CONV_HEAD, :]

    pa = _dot(h, win_ref[:, 0:2 * W])
    p_in = pa[:, :W]
    p_gate = pa[:, W:]
    pool_ext[POOL_HEAD:POOL_HEAD + ts, :] = p_in
    pos1 = (lax.broadcasted_iota(jnp.int32, (ts, POOL_GROUP_W), 0) + s * ts + 1).astype(F32)
    accs, cnts = [], []
    for g, win in enumerate(POOL_WINDOWS):
        c0 = g * POOL_GROUP_W
        level, shift = None, 1
        while shift < win:
            last = 2 * shift == win
            lo = POOL_HEAD if last else SUBLANES
            n = POOL_HEAD + ts - lo
            if level is None:
                cur = (pool_ext[lo:lo + n, c0:c0 + POOL_GROUP_W]
                       + pool_ext[lo - shift:lo - shift + n, c0:c0 + POOL_GROUP_W])
            else:
                cur = pool_tmp[level, lo:lo + n, :] + pool_tmp[level, lo - shift:lo - shift + n, :]
            if last:
                accs.append(cur)
            else:
                level = 0 if level is None else level + 1
                pool_tmp[level, lo:lo + n, :] = cur
            shift *= 2
        cnts.append(jnp.minimum(pos1, float(win)))
    a = _pool_finish(accs, p_in, p_gate, cnts, poolw_ref, pscale_ref)
    npool_ref[0] = pool_ext[POOL_HEAD + ts - POOL_BUF:POOL_HEAD + ts, :]
    pool_ext[POOL_PAD:POOL_HEAD, :] = pool_ext[ts + POOL_PAD:ts + POOL_HEAD, :]

    pc = _dot(h, win_ref[:, 5 * W:7 * W])
    q = pc[:, :W]
    x_gate = pc[:, W:]
    scale = 1.0 / math.sqrt(XHEAD_DIM)
    outs = []
    for hd in range(N_XHEADS):
        c0 = hd * XHEAD_DIM
        sc = _dot((q[:, c0:c0 + XHEAD_DIM] * scale).astype(BF16), kt_s[c0:c0 + XHEAD_DIM, :])
        e = jnp.exp(sc - jnp.max(sc, axis=-1, keepdims=True))
        l = jnp.sum(e, axis=-1, keepdims=True)
        outs.append(_dot(e.astype(BF16), v_s[:, c0:c0 + XHEAD_DIM]) / l)
    cattn = jnp.concatenate(outs, axis=-1) * _silu(x_gate)

    merged = None
    for n, br in ((0, a), (2, cattn), (1, bconv)):
        term = gate_s[:, n * D_MODEL:(n + 1) * D_MODEL] * _dot(br.astype(BF16), wbr_ref[n])
        merged = term if merged is None else merged + term
    y = _dot(merged.astype(BF16), wout_ref[...])
    y_ref[0] = x + _rmsnorm(y, gpost_ref[...])

    @pl.when(jnp.logical_and(pl.program_id(0) == pl.num_programs(0) - 1, s == pl.num_programs(1) - 1))
    def _():
        for copy in hand_over():
            copy.wait()


def _sample_kernel(x_ref, spool_ref, sconv_ref, k_ref, v_ref, gpre_ref, gpost_ref, win_ref,
                   poolw_ref, pscale_ref, convw_ref, convb_ref, lng_ref, lnb_ref, wbr_ref, wout_ref,
                   y_ref, npool_ref, nconv_ref, pool_ext, conv_ext, q_s, o_s, xy_s, v_buf, v_sem,
                   *, nseq, layer, first):
    W = BRANCH_W
    T = nseq * DEC_SEQ
    n_slabs = W // LANES
    step = pl.program_id(0)
    gpre_ref, gpost_ref, pscale_ref, convb_ref, lng_ref, lnb_ref = (
        _layer_row(r, layer) for r in (gpre_ref, gpost_ref, pscale_ref, convb_ref, lng_ref, lnb_ref))
    if first:
        _zero_other_layers(npool_ref, layer)
        _zero_other_layers(nconv_ref, layer)
    npool_ref = _own_layer(npool_ref, layer, first)
    nconv_ref = _own_layer(nconv_ref, layer, first)

    def v_copy(block):
        return pltpu.make_async_copy(v_ref.at[layer, pl.ds(block * nseq, nseq)], v_buf, v_sem)

    @pl.when(step == 0)
    def _():
        v_copy(0).start()

    x_seq = x_ref[...].reshape(T, D_MODEL)
    for c in range(D_MODEL // LANES):
        xy_s[c] = x_seq[:, c * LANES:(c + 1) * LANES]
    x = jnp.concatenate([
        jnp.concatenate([xy_s[c, pl.ds(t, nseq, stride=DEC_SEQ), :] for c in range(D_MODEL // LANES)], axis=-1)
        for t in range(DEC_SEQ)], axis=0)
    h = _rmsnorm(x, gpre_ref[...]).astype(BF16)

    pa = _dot(h, win_ref[:, 0:2 * W])
    p_in = pa[:, :W]
    p_gate = pa[:, W:]
    pos1 = (lax.broadcasted_iota(jnp.int32, (DEC_SEQ, nseq, POOL_GROUP_W), 0) + PAST_LEN + 1).astype(F32)
    pos1 = pos1.reshape(T, POOL_GROUP_W)
    accs, cnts = [], []
    for g, win in enumerate(POOL_WINDOWS):
        c0 = g * POOL_GROUP_W
        pool_ext[g, 0:POOL_BUF * nseq, :] = spool_ref[:, :, c0:c0 + POOL_GROUP_W].reshape(
            POOL_BUF * nseq, POOL_GROUP_W)
        pool_ext[g, POOL_BUF * nseq:(POOL_BUF + DEC_SEQ) * nseq, :] = p_in[:, c0:c0 + POOL_GROUP_W]
        acc = None
        for i in range(win):
            r0 = (POOL_BUF - i) * nseq
            rows = pool_ext[g, r0:r0 + T, :]
            acc = rows if acc is None else acc + rows
        accs.append(acc)
        cnts.append(jnp.minimum(pos1, float(win)))
    a = _pool_finish(accs, p_in, p_gate, cnts, poolw_ref, pscale_ref)
    for g in range(n_slabs):
        npool_ref[:, :, g * LANES:(g + 1) * LANES] = pool_ext[
            g, DEC_SEQ * nseq:(DEC_SEQ + POOL_BUF) * nseq, :].reshape(POOL_BUF, nseq, LANES)

    pb = _dot(h, win_ref[:, 2 * W:5 * W])
    u = pb[:, :W] * _sigmoid(pb[:, W:2 * W])
    c_gate = pb[:, 2 * W:]
    slabs = []
    for c in range(n_slabs):
        c0 = c * LANES
        conv_ext[c, 0:CONV_BUF * nseq, :] = sconv_ref[:, :, c0:c0 + LANES].reshape(CONV_BUF * nseq, LANES)
        conv_ext[c, CONV_BUF * nseq:(CONV_BUF + DEC_SEQ) * nseq, :] = u[:, c0:c0 + LANES]
        acc = None
        for k in range(CONV_WIDTH):
            term = conv_ext[c, k * nseq:k * nseq + T, :] * convw_ref[k:k + 1, c0:c0 + LANES]
            acc = term if acc is None else acc + term
        slabs.append(acc)
    bconv = _conv_finish(jnp.concatenate(slabs, axis=-1), c_gate, convb_ref, lng_ref, lnb_ref)
    for c in range(n_slabs):
        nconv_ref[:, :, c * LANES:(c + 1) * LANES] = conv_ext[
            c, DEC_SEQ * nseq:(DEC_SEQ + CONV_BUF) * nseq, :].reshape(CONV_BUF, nseq, LANES)

    pc = _dot(h, win_ref[:, 5 * W:7 * W])
    x_gate = pc[:, W:]
    for hd in range(N_XHEADS):
        q_s[hd] = pc[:, hd * XHEAD_DIM:(hd + 1) * XHEAD_DIM]
    scale = 1.0 / math.sqrt(XHEAD_DIM)
    nq = N_XHEADS * DEC_SEQ
    nk = N_MEM * N_XHEADS
    q4 = jnp.stack([
        jnp.concatenate([q_s[hd, pl.ds(sq, DEC_SEQ, stride=nseq), :] for hd in range(N_XHEADS)], axis=0)
        for sq in range(nseq)]).astype(BF16)
    sc = jnp.einsum('gqd,gkd->gqk', q4, k_ref[...].astype(BF16), preferred_element_type=F32) * scale
    q_head = lax.broadcasted_iota(jnp.int32, (nq, nk), 0) // DEC_SEQ
    k_head = lax.broadcasted_iota(jnp.int32, (nq, nk), 1) % N_XHEADS
    sc = jnp.where((q_head == k_head)[None], sc, MASKED_SCORE)
    e = jnp.exp(sc - jnp.max(sc, axis=-1, keepdims=True))
    l = jnp.sum(e, axis=-1, keepdims=True)
    v_copy(step).wait()
    o = jnp.einsum('gqk,gkd->gqd', e.astype(BF16), v_buf[...].astype(BF16),
                   preferred_element_type=F32) / l
    for sq in range(nseq):
        for hd in range(N_XHEADS):
            o_s[hd, pl.ds(sq, DEC_SEQ, stride=nseq), :] = o[sq, hd * DEC_SEQ:(hd + 1) * DEC_SEQ, :]
    cattn = jnp.concatenate([o_s[hd] for hd in range(N_XHEADS)], axis=-1) * _silu(x_gate)

    y = _merge_out(x, h, (a, bconv, cattn), win_ref, wbr_ref, wout_ref, gpost_ref)
    for c in range(D_MODEL // LANES):
        xy_s[c] = y[:, c * LANES:(c + 1) * LANES]
    for sq in range(nseq):
        y_ref[sq] = jnp.concatenate(
            [xy_s[c, pl.ds(sq, DEC_SEQ, stride=nseq), :] for c in range(D_MODEL // LANES)], axis=-1)

    @pl.when(step + 1 < pl.num_programs(0))
    def _():
        v_copy(step + 1).start()


def _layer_spec(shape, layer):
    zeros = (0,) * len(shape)
    return pl.BlockSpec((None,) + shape, lambda *_: (layer,) + zeros, pipeline_mode=pl.Buffered(1))


def _whole_spec(shape):
    zeros = (0,) * len(shape)
    return pl.BlockSpec(shape, lambda *_: zeros, pipeline_mode=pl.Buffered(1))


def _carry_layers(operands, in_specs, carried, first_output):
    aliases = {}
    for i, arr in enumerate(carried or ()):
        aliases[len(operands)] = first_output + i
        operands.append(arr)
        in_specs.append(pl.BlockSpec(memory_space=pl.ANY))
    return aliases


def _per_layer_spec(block, index_map, layer, first):
    if first:
        return pl.BlockSpec((DEPTH,) + block, lambda *i: (0,) + index_map(*i))
    return pl.BlockSpec((None,) + block, lambda *i: (layer,) + index_map(*i))


def _skip_carried(body, n_inputs, n_carried):
    def wrapped(*refs, **kw):
        return body(*refs[:n_inputs], *refs[n_inputs + n_carried:], **kw)
    return wrapped


def _prompt_layer(x, mem, layer, carried, gpre, gpost, gmem, wkv, win, poolw, pscale, convw, convb,
                  lng, lnb, wbr, wout):
    B, S, D = x.shape
    W = BRANCH_W
    ts = PROMPT_TILE
    grid = (B, S // ts)
    small = lambda width: _whole_spec((DEPTH, width))
    in_hbm = pl.BlockSpec(memory_space=pl.ANY)
    in_specs = [
        pl.BlockSpec((1, ts, D), lambda b, s: (b, s, 0)),
        pl.BlockSpec((1, N_MEM, D), lambda b, s: (b, 0, 0)),
        small(D), small(D), small(D),
        in_hbm, in_hbm,
        _layer_spec(poolw.shape[1:], layer),
        small(W),
        _layer_spec((CONV_WIDTH, W), layer),
        small(W), small(W), small(W),
        in_hbm, in_hbm,
    ]
    per_layer = [(POOL_BUF, W), (CONV_BUF, W), (N_MEM * N_XHEADS, XHEAD_DIM), (N_MEM * N_XHEADS, XHEAD_DIM)]
    first = carried is None
    out_specs = [pl.BlockSpec((1, ts, D), lambda b, s: (b, s, 0))] + [
        _per_layer_spec((1,) + shape, lambda b, s: (b, 0, 0), layer, first) for shape in per_layer]
    out_shape = [jax.ShapeDtypeStruct((B, S, D), F32)] + [
        jax.ShapeDtypeStruct((DEPTH, B) + shape, F32) for shape in per_layer]
    handed = [win.shape[1:], wbr.shape[1:], wout.shape[1:]]
    out_specs += [pl.BlockSpec(memory_space=pl.ANY)] * len(handed)
    out_shape += [jax.ShapeDtypeStruct(shape, BF16) for shape in handed]
    operands = [x, mem, gpre, gpost, gmem, wkv, win, poolw, pscale, convw, convb, lng, lnb, wbr, wout]
    aliases = _carry_layers(operands, in_specs, carried, first_output=1)
    scratch = [
        pltpu.VMEM((POOL_HEAD + ts, W), F32),
        pltpu.VMEM((len(POOL_WINDOWS) - 1, POOL_HEAD + ts, POOL_GROUP_W), F32),
        pltpu.VMEM((CONV_HEAD + ts, W), F32),
        pltpu.VMEM((SUBLANES - 1, ts + CONV_HEAD - SUBLANES, W), F32),
        pltpu.VMEM((ts, N_BRANCH * D_MODEL), F32),
        pltpu.VMEM((W, N_MEM), BF16),
        pltpu.VMEM((N_MEM, W), BF16),
        pltpu.VMEM(wkv.shape[1:], BF16),
        pltpu.VMEM(win.shape[1:], BF16),
        pltpu.VMEM(wbr.shape[1:], BF16),
        pltpu.VMEM(wout.shape[1:], BF16),
        pltpu.VMEM((STAGE_SLOTS, WIDE_STAGE_ROWS, win.shape[-1]), F32),
        pltpu.VMEM((STAGE_SLOTS, NARROW_STAGE_ROWS, D), F32),
        pltpu.SemaphoreType.DMA((STAGE_SLOTS,)),
        pltpu.SemaphoreType.DMA((STAGE_SLOTS,)),
        pltpu.SemaphoreType.DMA((len(handed),)),
    ]
    body = _skip_carried(_prompt_kernel, len(operands) - len(aliases), len(aliases))
    return pl.pallas_call(
        functools.partial(body, ts=ts, layer=layer, first=first),
        grid=grid, in_specs=in_specs, out_specs=out_specs, out_shape=out_shape,
        scratch_shapes=scratch, name=f"prompt_layer{layer}", input_output_aliases=aliases,
        compiler_params=pltpu.CompilerParams(
            dimension_semantics=("arbitrary", "arbitrary"), vmem_limit_bytes=VMEM_LIMIT),
    )(*operands)


def _sample_layer(x, spool, sconv, ck, cv, layer, carried, gpre, gpost, win, poolw, pscale, convw, convb,
                  lng, lnb, wbr, wout):
    nb, nt, D = x.shape
    W = BRANCH_W
    nseq = SAMPLE_SEQS
    T = nseq * nt
    n_slabs = W // LANES
    small = lambda width: _whole_spec((DEPTH, width))
    state_in = lambda rows: pl.BlockSpec((None, rows, nseq, W), lambda g: (layer, 0, g, 0))
    cache = pl.BlockSpec((None, nseq, N_MEM * N_XHEADS, XHEAD_DIM), lambda g: (layer, g, 0, 0))
    in_specs = [
        pl.BlockSpec((nseq, nt, D), lambda g: (g, 0, 0)),
        state_in(POOL_BUF), state_in(CONV_BUF), cache, pl.BlockSpec(memory_space=pl.ANY),
        small(D), small(D),
        _whole_spec(win.shape),
        _layer_spec(poolw.shape[1:], layer),
        small(W),
        _layer_spec((CONV_WIDTH, W), layer),
        small(W), small(W), small(W),
        _whole_spec(wbr.shape),
        _whole_spec(wout.shape),
    ]
    first = carried is None
    out_specs = [pl.BlockSpec((nseq, nt, D), lambda g: (g, 0, 0))] + [
        _per_layer_spec((rows, nseq, W), lambda g: (0, g, 0), layer, first) for rows in (POOL_BUF, CONV_BUF)]
    out_shape = [jax.ShapeDtypeStruct((nb, nt, D), F32)] + [
        jax.ShapeDtypeStruct((DEPTH, rows, nb, W), F32) for rows in (POOL_BUF, CONV_BUF)]
    operands = [x, spool, sconv, ck, cv, gpre, gpost, win, poolw, pscale, convw, convb, lng, lnb, wbr, wout]
    aliases = _carry_layers(operands, in_specs, carried, first_output=1)
    scratch = [
        pltpu.VMEM((n_slabs, (POOL_BUF + nt) * nseq, LANES), F32),
        pltpu.VMEM((n_slabs, (CONV_BUF + nt) * nseq, LANES), F32),
        pltpu.VMEM((N_XHEADS, T, XHEAD_DIM), F32),
        pltpu.VMEM((N_XHEADS, T, XHEAD_DIM), F32),
        pltpu.VMEM((D // LANES, T, LANES), F32),
        pltpu.VMEM((nseq, N_MEM * N_XHEADS, XHEAD_DIM), F32),
        pltpu.SemaphoreType.DMA(()),
    ]
    body = _skip_carried(_sample_kernel, len(operands) - len(aliases), len(aliases))
    return pl.pallas_call(
        functools.partial(body, nseq=nseq, layer=layer, first=first),
        grid=(nb // nseq,), in_specs=in_specs, out_specs=out_specs, out_shape=out_shape,
        scratch_shapes=scratch, name=f"sample_layer{layer}", input_output_aliases=aliases,
        compiler_params=pltpu.CompilerParams(
            dimension_semantics=("arbitrary",), vmem_limit_bytes=VMEM_LIMIT),
    )(*operands)


def kernel(x_prompt, x_sample, state_pool, state_conv, cache_mem_k, cache_mem_v, mem_prompt,
           norm_pre, norm_post, mem_norm, w_mem_kv, w_in, pool_w, pool_scale, conv_w, conv_b,
           conv_ln_g, conv_ln_b, w_branch, w_out):
    W = BRANCH_W
    nb, nt, D = x_sample.shape
    gpre, gpost, gmem = norm_pre, norm_post, mem_norm
    pscale, convb, lng, lnb = pool_scale, conv_b, conv_ln_g, conv_ln_b
    ck = cache_mem_k.reshape(DEPTH, nb, N_MEM * N_XHEADS, XHEAD_DIM)
    cv = cache_mem_v.reshape(DEPTH, nb, N_MEM * N_XHEADS, XHEAD_DIM)

    spool_t = state_pool.transpose(0, 2, 1, 3)
    sconv_t = state_conv.transpose(0, 2, 1, 3)
    xs = x_sample

    xp = x_prompt
    prompt_new, sample_new = None, None
    for i in range(DEPTH):
        xp, *prompt_new = _prompt_layer(
            xp, mem_prompt, i, prompt_new, gpre, gpost, gmem, w_mem_kv, w_in, pool_w, pscale, conv_w,
            convb, lng, lnb, w_branch, w_out)
        win, wbr, wout = prompt_new[-3:]
        prompt_new = prompt_new[:-3]
        xs, *sample_new = _sample_layer(
            xs, spool_t, sconv_t, ck, cv, i, sample_new, gpre, gpost, win, pool_w, pscale, conv_w,
            convb, lng, lnb, wbr, wout)
    pool_p, conv_p, mk_p, mv_p = prompt_new
    pool_s, conv_s = (st.transpose(0, 2, 1, 3) for st in sample_new)
    heads = lambda kv: kv.reshape(DEPTH, kv.shape[1], N_MEM, N_XHEADS, XHEAD_DIM)
    return (xp, xs, pool_p, conv_p, heads(mk_p), heads(mv_p), pool_s, conv_s)
```

```python
import functools
import math

import jax
import jax.numpy as jnp
from jax import lax
from jax.experimental import pallas as pl
from jax.experimental.pallas import tpu as pltpu

D_MODEL = 1024
DEPTH = 2
DEC_SEQ = 8
PAST_LEN = 16384
BRANCH_W = D_MODEL // 2
N_BRANCH = 3
POOL_WINDOWS = (2, 4, 8, 16)
POOL_GROUP_W = BRANCH_W // len(POOL_WINDOWS)
POOL_BUF = max(POOL_WINDOWS) - 1
CONV_WIDTH = 31
CONV_BUF = CONV_WIDTH - 1
N_MEM = 256
N_XHEADS = 4
XHEAD_DIM = BRANCH_W // N_XHEADS
N_IN_SLICES = 7
EPS = 1e-6

LANES = 128
SUBLANES = 8
POOL_PAD = 8
POOL_HEAD = POOL_PAD + 16
CONV_HEAD = 32
PROMPT_TILE = 256
SAMPLE_SEQS = 16
STAGE_SLOTS = 4
WIDE_STAGE_ROWS = 64
NARROW_STAGE_ROWS = 256
VMEM_LIMIT = 60 * 1024 * 1024
NEG_LOG2E = -1.0 / math.log(2.0)
MASKED_SCORE = float("-inf")

F32 = jnp.float32
BF16 = jnp.bfloat16


def _sigmoid(x):
    return 1.0 / (1.0 + jnp.exp2(x * NEG_LOG2E))


def _silu(x):
    return x * _sigmoid(x)


def _rmsnorm(x, g):
    return x * lax.rsqrt(jnp.mean(x * x, axis=-1, keepdims=True) + EPS) * g


def _dot(a, b):
    return jnp.dot(a, b, preferred_element_type=F32)


def _pool_finish(acc_parts, p_in, p_gate, cnt_parts, poolw_ref, pscale_ref):
    ys = []
    for g in range(len(POOL_WINDOWS)):
        c0 = g * POOL_GROUP_W
        mixed = acc_parts[g] / cnt_parts[g] - p_in[:, c0:c0 + POOL_GROUP_W]
        ys.append(_dot(mixed.astype(BF16), poolw_ref[g].astype(BF16)))
    y = jnp.concatenate(ys, axis=-1)
    return y * pscale_ref[...] * _silu(p_gate)


def _conv_finish(cv, c_gate, convb_ref, lng_ref, lnb_ref):
    cv = cv + convb_ref[...]
    mu = jnp.mean(cv, axis=-1, keepdims=True)
    d = cv - mu
    var = jnp.mean(d * d, axis=-1, keepdims=True)
    r = d * lax.rsqrt(var + EPS) * lng_ref[...] + lnb_ref[...]
    return _silu(r) * _silu(c_gate)


def _merge_out(x, h, branches, win_ref, wbr_ref, wout_ref, gpost_ref):
    merged = None
    base = N_IN_SLICES * BRANCH_W
    for n, br in enumerate(branches):
        proj = _dot(br.astype(BF16), wbr_ref[n])
        gate = _sigmoid(_dot(h, win_ref[:, base + n * D_MODEL:base + (n + 1) * D_MODEL]))
        term = gate * proj
        merged = term if merged is None else merged + term
    y = _dot(merged.astype(BF16), wout_ref[...])
    return x + _rmsnorm(y, gpost_ref[...])


def _layer_row(ref, layer):
    return ref.at[pl.ds(layer, 1)]


def _own_layer(ref, layer, first):
    return ref.at[layer] if first else ref


def _zero_other_layers(ref, layer):
    for other in range(ref.shape[0]):
        if other != layer:
            ref[other] = jnp.zeros(ref.shape[1:], ref.dtype)


def _stage_and_cast(streams):
    copies = [[pltpu.make_async_copy(src, stage.at[i % stage.shape[0]], sem.at[i % stage.shape[0]])
               for i, (src, _) in enumerate(chunks)] for chunks, stage, sem in streams]
    for stream, (_, stage, _) in zip(copies, streams):
        for copy in stream[:stage.shape[0] - 1]:
            copy.start()
    for i in range(max(len(stream) for stream in copies)):
        for stream, (_, stage, _) in zip(copies, streams):
            ahead = i + stage.shape[0] - 1
            if ahead < len(stream):
                stream[ahead].start()
        for stream, (chunks, stage, _) in zip(copies, streams):
            if i < len(stream):
                stream[i].wait()
                chunks[i][1][...] = stage[i % stage.shape[0]].astype(BF16)


def _prompt_kernel(x_ref, mem_ref, gpre_ref, gpost_ref, gmem_ref, wkv_hbm, win_hbm, poolw_ref,
                   pscale_ref, convw_ref, convb_ref, lng_ref, lnb_ref, wbr_hbm, wout_hbm,
                   y_ref, npool_ref, nconv_ref, mk_ref, mv_ref, win_out, wbr_out, wout_out,
                   pool_ext, pool_tmp, conv_ext, conv_sh, gate_s, kt_s, v_s,
                   wkv_ref, win_ref, wbr_ref, wout_ref, stage_wide, stage_narrow, sem_wide, sem_narrow,
                   sem_out, *, ts, layer, first):
    W = BRANCH_W
    s = pl.program_id(1)
    first_step = jnp.logical_and(pl.program_id(0) == 0, s == 0)
    gpre_ref, gpost_ref, gmem_ref, pscale_ref, convb_ref, lng_ref, lnb_ref = (
        _layer_row(r, layer) for r in (gpre_ref, gpost_ref, gmem_ref, pscale_ref, convb_ref, lng_ref, lnb_ref))
    per_layer = (npool_ref, nconv_ref, mk_ref, mv_ref)
    npool_ref, nconv_ref, mk_ref, mv_ref = (_own_layer(r, layer, first) for r in per_layer)

    def hand_over():
        return [pltpu.make_async_copy(src, dst, sem_out.at[i]) for i, (src, dst) in enumerate(
            ((win_ref, win_out), (wbr_ref, wbr_out), (wout_ref, wout_out)))]

    @pl.when(first_step)
    def _():
        nrows = stage_narrow.shape[1]
        narrow = [(wkv_hbm.at[layer, pl.ds(r, nrows), :], wkv_ref.at[pl.ds(r, nrows), :])
                  for r in range(0, D_MODEL, nrows)]
        narrow += [(wbr_hbm.at[layer, n, pl.ds(r, nrows), :], wbr_ref.at[n, pl.ds(r, nrows), :])
                   for n in range(N_BRANCH) for r in range(0, W, nrows)]
        narrow += [(wout_hbm.at[layer, pl.ds(r, nrows), :], wout_ref.at[pl.ds(r, nrows), :])
                   for r in range(0, D_MODEL, nrows)]
        wrows = stage_wide.shape[1]
        wide = [(win_hbm.at[layer, pl.ds(r, wrows), :], win_ref.at[pl.ds(r, wrows), :])
                for r in range(0, D_MODEL, wrows)]
        _stage_and_cast([(narrow, stage_narrow, sem_narrow), (wide, stage_wide, sem_wide)])
        for copy in hand_over():
            copy.start()

    @pl.when(s == 0)
    def _():
        if first:
            for r in per_layer:
                _zero_other_layers(r, layer)
        pool_ext[0:POOL_HEAD, :] = jnp.zeros((POOL_HEAD, W), F32)
        pool_tmp[:, 0:POOL_PAD, :] = jnp.zeros((pool_tmp.shape[0], POOL_PAD, POOL_GROUP_W), F32)
        conv_ext[0:CONV_HEAD, :] = jnp.zeros((CONV_HEAD, W), F32)
        hm = _rmsnorm(mem_ref[0], gmem_ref[...]).astype(BF16)
        kv = _dot(hm, wkv_ref[...])
        k = kv[:, :W]
        v = kv[:, W:]
        for hd in range(N_XHEADS):
            rows = pl.ds(hd, N_MEM, stride=N_XHEADS)
            mk_ref[0, rows, :] = k[:, hd * XHEAD_DIM:(hd + 1) * XHEAD_DIM]
            mv_ref[0, rows, :] = v[:, hd * XHEAD_DIM:(hd + 1) * XHEAD_DIM]
        kt_s[...] = k.T.astype(BF16)
        v_s[...] = v.astype(BF16)

    x = x_ref[0]
    h = _rmsnorm(x, gpre_ref[...]).astype(BF16)

    pb = _dot(h, win_ref[:, 2 * W:5 * W])
    u = pb[:, :W] * _sigmoid(pb[:, W:2 * W])
    c_gate = pb[:, 2 * W:]
    conv_ext[CONV_HEAD:CONV_HEAD + ts, :] = u
    for r in range(1, SUBLANES):
        conv_sh[r - 1, :, :] = conv_ext[r:r + ts + CONV_HEAD - SUBLANES, :]

    off = CONV_HEAD - CONV_BUF
    gate_base = N_IN_SLICES * W
    n_slabs = W // LANES
    gate_chunk = N_BRANCH * D_MODEL // n_slabs
    slabs = []
    for c in range(n_slabs):
        g0 = c * gate_chunk
        gate_s[:, g0:g0 + gate_chunk] = _sigmoid(
            _dot(h, win_ref[:, gate_base + g0:gate_base + g0 + gate_chunk]))
        c0 = c * LANES
        acc = None
        for k in range(CONV_WIDTH):
            a8, r = divmod(off + k, SUBLANES)
            if r == 0:
                rows = conv_ext[a8 * SUBLANES:a8 * SUBLANES + ts, c0:c0 + LANES]
            else:
                rows = conv_sh[r - 1, a8 * SUBLANES:a8 * SUBLANES + ts, c0:c0 + LANES]
            term = rows * convw_ref[k:k + 1, c0:c0 + LANES]
            acc = term if acc is None else acc + term
        slabs.append(acc)
    bconv = _conv_finish(jnp.concatenate(slabs, axis=-1), c_gate, convb_ref, lng_ref, lnb_ref)
    nconv_ref[0] = conv_ext[CONV_HEAD + ts - CONV_BUF:CONV_HEAD + ts, :]
    conv_ext[0:CONV_HEAD, :] = conv_ext[ts:ts + CONV_HEAD, :]

    pc = _dot(h, win_ref[:, 5 * W:7 * W])
    q = pc[:, :W]
    x_gate = pc[:, W:]
    scale = 1.0 / math.sqrt(XHEAD_DIM)
    outs = []
    for hd in range(N_XHEADS):
        c0 = hd * XHEAD_DIM
        sc = _dot((q[:, c0:c0 + XHEAD_DIM] * scale).astype(BF16), kt_s[c0:c0 + XHEAD_DIM, :])
        e = jnp.exp(sc - jnp.max(sc, axis=-1, keepdims=True))
        l = jnp.sum(e, axis=-1, keepdims=True)
        outs.append(_dot(e.astype(BF16), v_s[:, c0:c0 + XHEAD_DIM]) / l)
    cattn = jnp.concatenate(outs, axis=-1) * _silu(x_gate)

    pa = _dot(h, win_ref[:, 0:2 * W])
    p_in = pa[:, :W]
    p_gate = pa[:, W:]
    pool_ext[POOL_HEAD:POOL_HEAD + ts, :] = p_in
    pos1 = (lax.broadcasted_iota(jnp.int32, (ts, POOL_GROUP_W), 0) + s * ts + 1).astype(F32)
    accs, cnts = [], []
    for g, win in enumerate(POOL_WINDOWS):
        c0 = g * POOL_GROUP_W
        level, shift = None, 1
        while shift < win:
            last = 2 * shift == win
            lo = POOL_HEAD if last else SUBLANES
            n = POOL_HEAD + ts - lo
            if level is None:
                cur = (pool_ext[lo:lo + n, c0:c0 + POOL_GROUP_W]
                       + pool_ext[lo - shift:lo - shift + n, c0:c0 + POOL_GROUP_W])
            else:
                cur = pool_tmp[level, lo:lo + n, :] + pool_tmp[level, lo - shift:lo - shift + n, :]
            if last:
                accs.append(cur)
            else:
                level = 0 if level is None else level + 1
                pool_tmp[level, lo:lo + n, :] = cur
            shift *= 2
        cnts.append(jnp.minimum(pos1, float(win)))
    a = _pool_finish(accs, p_in, p_gate, cnts, poolw_ref, pscale_ref)
    npool_ref[0] = pool_ext[POOL_HEAD + ts - POOL_BUF:POOL_HEAD + ts, :]
    pool_ext[POOL_PAD:POOL_HEAD, :] = pool_ext[ts + POOL_PAD:ts + POOL_HEAD, :]

    merged = None
    for n, br in ((2, cattn), (0, a), (1, bconv)):
        term = gate_s[:, n * D_MODEL:(n + 1) * D_MODEL] * _dot(br.astype(BF16), wbr_ref[n])
        merged = term if merged is None else merged + term
    y = _dot(merged.astype(BF16), wout_ref[...])
    y_ref[0] = x + _rmsnorm(y, gpost_ref[...])

    @pl.when(jnp.logical_and(pl.program_id(0) == pl.num_programs(0) - 1, s == pl.num_programs(1) - 1))
    def _():
        for copy in hand_over():
            copy.wait()


def _sample_kernel(x_ref, spool_ref, sconv_ref, k_ref, v_ref, gpre_ref, gpost_ref, win_ref,
                   poolw_ref, pscale_ref, convw_ref, convb_ref, lng_ref, lnb_ref, wbr_ref, wout_ref,
                   y_ref, npool_ref, nconv_ref, pool_ext, conv_ext, q_s, o_s, xy_s, v_buf, v_sem,
                   *, nseq, layer, first):
    W = BRANCH_W
    T = nseq * DEC_SEQ
    n_slabs = W // LANES
    step = pl.program_id(0)
    gpre_ref, gpost_ref, pscale_ref, convb_ref, lng_ref, lnb_ref = (
        _layer_row(r, layer) for r in (gpre_ref, gpost_ref, pscale_ref, convb_ref, lng_ref, lnb_ref))
    if first:
        _zero_other_layers(npool_ref, layer)
        _zero_other_layers(nconv_ref, layer)
    npool_ref = _own_layer(npool_ref, layer, first)
    nconv_ref = _own_layer(nconv_ref, layer, first)

    def v_copy(block):
        return pltpu.make_async_copy(v_ref.at[layer, pl.ds(block * nseq, nseq)], v_buf, v_sem)

    @pl.when(step == 0)
    def _():
        v_copy(0).start()

    x_seq = x_ref[...].reshape(T, D_MODEL)
    for c in range(D_MODEL // LANES):
        xy_s[c] = x_seq[:, c * LANES:(c + 1) * LANES]
    x = jnp.concatenate([
        jnp.concatenate([xy_s[c, pl.ds(t, nseq, stride=DEC_SEQ), :] for c in range(D_MODEL // LANES)], axis=-1)
        for t in range(DEC_SEQ)], axis=0)
    h = _rmsnorm(x, gpre_ref[...]).astype(BF16)

    pa = _dot(h, win_ref[:, 0:2 * W])
    p_in = pa[:, :W]
    p_gate = pa[:, W:]
    pos1 = (lax.broadcasted_iota(jnp.int32, (DEC_SEQ, nseq, POOL_GROUP_W), 0) + PAST_LEN + 1).astype(F32)
    pos1 = pos1.reshape(T, POOL_GROUP_W)
    accs, cnts = [], []
    for g, win in enumerate(POOL_WINDOWS):
        c0 = g * POOL_GROUP_W
        pool_ext[g, 0:POOL_BUF * nseq, :] = spool_ref[:, :, c0:c0 + POOL_GROUP_W].reshape(
            POOL_BUF * nseq, POOL_GROUP_W)
        pool_ext[g, POOL_BUF * nseq:(POOL_BUF + DEC_SEQ) * nseq, :] = p_in[:, c0:c0 + POOL_GROUP_W]
        acc = None
        for i in range(win):
            r0 = (POOL_BUF - i) * nseq
            rows = pool_ext[g, r0:r0 + T, :]
            acc = rows if acc is None else acc + rows
        accs.append(acc)
        cnts.append(jnp.minimum(pos1, float(win)))
    a = _pool_finish(accs, p_in, p_gate, cnts, poolw_ref, pscale_ref)
    for g in range(n_slabs):
        npool_ref[:, :, g * LANES:(g + 1) * LANES] = pool_ext[
            g, DEC_SEQ * nseq:(DEC_SEQ + POOL_BUF) * nseq, :].reshape(POOL_BUF, nseq, LANES)

    pb = _dot(h, win_ref[:, 2 * W:5 * W])
    u = pb[:, :W] * _sigmoid(pb[:, W:2 * W])
    c_gate = pb[:, 2 * W:]
    slabs = []
    for c in range(n_slabs):
        c0 = c * LANES
        conv_ext[c, 0:CONV_BUF * nseq, :] = sconv_ref[:, :, c0:c0 + LANES].reshape(CONV_BUF * nseq, LANES)
        conv_ext[c, CONV_BUF * nseq:(CONV_BUF + DEC_SEQ) * nseq, :] = u[:, c0:c0 + LANES]
        acc = None
        for k in range(CONV_WIDTH):
            term = conv_ext[c, k * nseq:k * nseq + T, :] * convw_ref[k:k + 1, c0:c0 + LANES]
            acc = term if acc is None else acc + term
        slabs.append(acc)
    bconv = _conv_finish(jnp.concatenate(slabs, axis=-1), c_gate, convb_ref, lng_ref, lnb_ref)
    for c in range(n_slabs):
        nconv_ref[:, :, c * LANES:(c + 1) * LANES] = conv_ext[
            c, DEC_SEQ * nseq:(DEC_SEQ + CONV_BUF) * nseq, :].reshape(CONV_BUF, nseq, LANES)

    pc = _dot(h, win_ref[:, 5 * W:7 * W])
    x_gate = pc[:, W:]
    for hd in range(N_XHEADS):
        q_s[hd] = pc[:, hd * XHEAD_DIM:(hd + 1) * XHEAD_DIM]
    scale = 1.0 / math.sqrt(XHEAD_DIM)
    nq = N_XHEADS * DEC_SEQ
    nk = N_MEM * N_XHEADS
    q4 = jnp.stack([
        jnp.concatenate([q_s[hd, pl.ds(sq, DEC_SEQ, stride=nseq), :] for hd in range(N_XHEADS)], axis=0)
        for sq in range(nseq)]).astype(BF16)
    sc = jnp.einsum('gqd,gkd->gqk', q4, k_ref[...].astype(BF16), preferred_element_type=F32) * scale
    q_head = lax.broadcasted_iota(jnp.int32, (nq, nk), 0) // DEC_SEQ
    k_head = lax.broadcasted_iota(jnp.int32, (nq, nk), 1) % N_XHEADS
    sc = jnp.where((q_head == k_head)[None], sc, MASKED_SCORE)
    e = jnp.exp(sc - jnp.max(sc, axis=-1, keepdims=True))
    l = jnp.sum(e, axis=-1, keepdims=True)
    v_copy(step).wait()
    o = jnp.einsum('gqk,gkd->gqd', e.astype(BF16), v_buf[...].astype(BF16),
                   preferred_element_type=F32) / l
    for sq in range(nseq):
        for hd in range(N_XHEADS):
            o_s[hd, pl.ds(sq, DEC_SEQ, stride=nseq), :] = o[sq, hd * DEC_SEQ:(hd + 1) * DEC_SEQ, :]
    cattn = jnp.concatenate([o_s[hd] for hd in range(N_XHEADS)], axis=-1) * _silu(x_gate)

    y = _merge_out(x, h, (a, bconv, cattn), win_ref, wbr_ref, wout_ref, gpost_ref)
    for c in range(D_MODEL // LANES):
        xy_s[c] = y[:, c * LANES:(c + 1) * LANES]
    for sq in range(nseq):
        y_ref[sq] = jnp.concatenate(
            [xy_s[c, pl.ds(sq, DEC_SEQ, stride=nseq), :] for c in range(D_MODEL // LANES)], axis=-1)

    @pl.when(step + 1 < pl.num_programs(0))
    def _():
        v_copy(step + 1).start()


def _layer_spec(shape, layer):
    zeros = (0,) * len(shape)
    return pl.BlockSpec((None,) + shape, lambda *_: (layer,) + zeros, pipeline_mode=pl.Buffered(1))


def _whole_spec(shape):
    zeros = (0,) * len(shape)
    return pl.BlockSpec(shape, lambda *_: zeros, pipeline_mode=pl.Buffered(1))


def _carry_layers(operands, in_specs, carried, first_output):
    aliases = {}
    for i, arr in enumerate(carried or ()):
        aliases[len(operands)] = first_output + i
        operands.append(arr)
        in_specs.append(pl.BlockSpec(memory_space=pl.ANY))
    return aliases


def _per_layer_spec(block, index_map, layer, first):
    if first:
        return pl.BlockSpec((DEPTH,) + block, lambda *i: (0,) + index_map(*i))
    return pl.BlockSpec((None,) + block, lambda *i: (layer,) + index_map(*i))


def _skip_carried(body, n_inputs, n_carried):
    def wrapped(*refs, **kw):
        return body(*refs[:n_inputs], *refs[n_inputs + n_carried:], **kw)
    return wrapped


def _prompt_layer(x, mem, layer, carried, gpre, gpost, gmem, wkv, win, poolw, pscale, convw, convb,
                  lng, lnb, wbr, wout):
    B, S, D = x.shape
    W = BRANCH_W
    ts = PROMPT_TILE
    grid = (B, S // ts)
    small = lambda width: _whole_spec((DEPTH, width))
    in_hbm = pl.BlockSpec(memory_space=pl.ANY)
    in_specs = [
        pl.BlockSpec((1, ts, D), lambda b, s: (b, s, 0)),
        pl.BlockSpec((1, N_MEM, D), lambda b, s: (b, 0, 0)),
        small(D), small(D), small(D),
        in_hbm, in_hbm,
        _layer_spec(poolw.shape[1:], layer),
        small(W),
        _layer_spec((CONV_WIDTH, W), layer),
        small(W), small(W), small(W),
        in_hbm, in_hbm,
    ]
    per_layer = [(POOL_BUF, W), (CONV_BUF, W), (N_MEM * N_XHEADS, XHEAD_DIM), (N_MEM * N_XHEADS, XHEAD_DIM)]
    first = carried is None
    out_specs = [pl.BlockSpec((1, ts, D), lambda b, s: (b, s, 0))] + [
        _per_layer_spec((1,) + shape, lambda b, s: (b, 0, 0), layer, first) for shape in per_layer]
    out_shape = [jax.ShapeDtypeStruct((B, S, D), F32)] + [
        jax.ShapeDtypeStruct((DEPTH, B) + shape, F32) for shape in per_layer]
    handed = [win.shape[1:], wbr.shape[1:], wout.shape[1:]]
    out_specs += [pl.BlockSpec(memory_space=pl.ANY)] * len(handed)
    out_shape += [jax.ShapeDtypeStruct(shape, BF16) for shape in handed]
    operands = [x, mem, gpre, gpost, gmem, wkv, win, poolw, pscale, convw, convb, lng, lnb, wbr, wout]
    aliases = _carry_layers(operands, in_specs, carried, first_output=1)
    scratch = [
        pltpu.VMEM((POOL_HEAD + ts, W), F32),
        pltpu.VMEM((len(POOL_WINDOWS) - 1, POOL_HEAD + ts, POOL_GROUP_W), F32),
        pltpu.VMEM((CONV_HEAD + ts, W), F32),
        pltpu.VMEM((SUBLANES - 1, ts + CONV_HEAD - SUBLANES, W), F32),
        pltpu.VMEM((ts, N_BRANCH * D_MODEL), F32),
        pltpu.VMEM((W, N_MEM), BF16),
        pltpu.VMEM((N_MEM, W), BF16),
        pltpu.VMEM(wkv.shape[1:], BF16),
        pltpu.VMEM(win.shape[1:], BF16),
        pltpu.VMEM(wbr.shape[1:], BF16),
        pltpu.VMEM(wout.shape[1:], BF16),
        pltpu.VMEM((STAGE_SLOTS, WIDE_STAGE_ROWS, win.shape[-1]), F32),
        pltpu.VMEM((STAGE_SLOTS, NARROW_STAGE_ROWS, D), F32),
        pltpu.SemaphoreType.DMA((STAGE_SLOTS,)),
        pltpu.SemaphoreType.DMA((STAGE_SLOTS,)),
        pltpu.SemaphoreType.DMA((len(handed),)),
    ]
    body = _skip_carried(_prompt_kernel, len(operands) - len(aliases), len(aliases))
    return pl.pallas_call(
        functools.partial(body, ts=ts, layer=layer, first=first),
        grid=grid, in_specs=in_specs, out_specs=out_specs, out_shape=out_shape,
        scratch_shapes=scratch, name=f"prompt_layer{layer}", input_output_aliases=aliases,
        compiler_params=pltpu.CompilerParams(
            dimension_semantics=("arbitrary", "arbitrary"), vmem_limit_bytes=VMEM_LIMIT),
    )(*operands)


def _sample_layer(x, spool, sconv, ck, cv, layer, carried, gpre, gpost, win, poolw, pscale, convw, convb,
                  lng, lnb, wbr, wout):
    nb, nt, D = x.shape
    W = BRANCH_W
    nseq = SAMPLE_SEQS
    T = nseq * nt
    n_slabs = W // LANES
    small = lambda width: _whole_spec((DEPTH, width))
    state_in = lambda rows: pl.BlockSpec((None, rows, nseq, W), lambda g: (layer, 0, g, 0))
    cache = pl.BlockSpec((None, nseq, N_MEM * N_XHEADS, XHEAD_DIM), lambda g: (layer, g, 0, 0))
    in_specs = [
        pl.BlockSpec((nseq, nt, D), lambda g: (g, 0, 0)),
        state_in(POOL_BUF), state_in(CONV_BUF), cache, pl.BlockSpec(memory_space=pl.ANY),
        small(D), small(D),
        _whole_spec(win.shape),
        _layer_spec(poolw.shape[1:], layer),
        small(W),
        _layer_spec((CONV_WIDTH, W), layer),
        small(W), small(W), small(W),
        _whole_spec(wbr.shape),
        _whole_spec(wout.shape),
    ]
    first = carried is None
    out_specs = [pl.BlockSpec((nseq, nt, D), lambda g: (g, 0, 0))] + [
        _per_layer_spec((rows, nseq, W), lambda g: (0, g, 0), layer, first) for rows in (POOL_BUF, CONV_BUF)]
    out_shape = [jax.ShapeDtypeStruct((nb, nt, D), F32)] + [
        jax.ShapeDtypeStruct((DEPTH, rows, nb, W), F32) for rows in (POOL_BUF, CONV_BUF)]
    operands = [x, spool, sconv, ck, cv, gpre, gpost, win, poolw, pscale, convw, convb, lng, lnb, wbr, wout]
    aliases = _carry_layers(operands, in_specs, carried, first_output=1)
    scratch = [
        pltpu.VMEM((n_slabs, (POOL_BUF + nt) * nseq, LANES), F32),
        pltpu.VMEM((n_slabs, (CONV_BUF + nt) * nseq, LANES), F32),
        pltpu.VMEM((N_XHEADS, T, XHEAD_DIM), F32),
        pltpu.VMEM((N_XHEADS, T, XHEAD_DIM), F32),
        pltpu.VMEM((D // LANES, T, LANES), F32),
        pltpu.VMEM((nseq, N_MEM * N_XHEADS, XHEAD_DIM), F32),
        pltpu.SemaphoreType.DMA(()),
    ]
    body = _skip_carried(_sample_kernel, len(operands) - len(aliases), len(aliases))
    return pl.pallas_call(
        functools.partial(body, nseq=nseq, layer=layer, first=first),
        grid=(nb // nseq,), in_specs=in_specs, out_specs=out_specs, out_shape=out_shape,
        scratch_shapes=scratch, name=f"sample_layer{layer}", input_output_aliases=aliases,
        compiler_params=pltpu.CompilerParams(
            dimension_semantics=("arbitrary",), vmem_limit_bytes=VMEM_LIMIT),
    )(*operands)


def kernel(x_prompt, x_sample, state_pool, state_conv, cache_mem_k, cache_mem_v, mem_prompt,
           norm_pre, norm_post, mem_norm, w_mem_kv, w_in, pool_w, pool_scale, conv_w, conv_b,
           conv_ln_g, conv_ln_b, w_branch, w_out):
    W = BRANCH_W
    nb, nt, D = x_sample.shape
    gpre, gpost, gmem = norm_pre, norm_post, mem_norm
    pscale, convb, lng, lnb = pool_scale, conv_b, conv_ln_g, conv_ln_b
    ck = cache_mem_k.reshape(DEPTH, nb, N_MEM * N_XHEADS, XHEAD_DIM)
    cv = cache_mem_v.reshape(DEPTH, nb, N_MEM * N_XHEADS, XHEAD_DIM)

    spool_t = state_pool.transpose(0, 2, 1, 3)
    sconv_t = state_conv.transpose(0, 2, 1, 3)
    xs = x_sample

    xp = x_prompt
    prompt_new, sample_new = None, None
    for i in range(DEPTH):
        xp, *prompt_new = _prompt_layer(
            xp, mem_prompt, i, prompt_new, gpre, gpost, gmem, w_mem_kv, w_in, pool_w, pscale, conv_w,
            convb, lng, lnb, w_branch, w_out)
        win, wbr, wout = prompt_new[-3:]
        prompt_new = prompt_new[:-3]
        xs, *sample_new = _sample_layer(
            xs, spool_t, sconv_t, ck, cv, i, sample_new, gpre, gpost, win, pool_w, pscale, conv_w,
            convb, lng, lnb, wbr, wout)
    pool_p, conv_p, mk_p, mv_p = prompt_new
    pool_s, conv_s = (st.transpose(0, 2, 1, 3) for st in sample_new)
    heads = lambda kv: kv.reshape(DEPTH, kv.shape[1], N_MEM, N_XHEADS, XHEAD_DIM)
    return (xp, xs, pool_p, conv_p, heads(mk_p), heads(mv_p), pool_s, conv_s)
```
